```python
import jax, jax.numpy as jnp
from jax import lax
import numpy as np

D_MODEL = 1024
BATCH = 2
SEQ = 8192
DEPTH = 1
DEC_BATCH = 128
DEC_SEQ = 4
PAST_LEN = 8192
PAGE_SIZE = 128

N_HEADS = 8
QK_NOPE = 128
QK_ROPE = 64
V_HEAD = 128
Q_LORA = 384
KV_LORA = 256
ROPE_THETA = 10000.0
Q_BLOCK = 128
SM_SCALE = (QK_NOPE + QK_ROPE) ** -0.5
CONV_DIM = D_MODEL
CONV_W = 31
N_EXPERTS = 256
TOP_K = 8
N_GROUPS = 8
TOPK_GROUPS = 4
D_EXPERT = D_MODEL // 4
D_SHARED = D_MODEL // 4
ROUTED_SCALE = 2.5
EXPERT_BLOCK = 128
PLE_DIM = 256
ALPHA = (2 * DEPTH) ** 0.25
BETA = (8 * DEPTH) ** -0.25
LN_EPS = 1e-5
RMS_EPS = 1e-6
IN_COLS = Q_LORA + KV_LORA + QK_ROPE + 2 * CONV_DIM + 2 * D_MODEL
CG_COLS = 2 * CONV_DIM + 2 * D_MODEL

kernel_name = "mla_conformer_conv_moe_hybrid_step"


def _layernorm(x, g, b):
    xf = x.astype(jnp.float32)
    mu = xf.mean(-1, keepdims=True)
    xc = xf - mu
    var = (xc * xc).mean(-1, keepdims=True)
    return (xc * lax.rsqrt(var + LN_EPS) * g.astype(jnp.float32) + b.astype(jnp.float32)).astype(x.dtype)


def _rmsnorm(x, g):
    xf = x.astype(jnp.float32)
    return (xf * lax.rsqrt((xf * xf).mean(-1, keepdims=True) + RMS_EPS) * g.astype(jnp.float32)).astype(x.dtype)


def _rope(x, pos):
    half = x.shape[-1] // 2
    inv = ROPE_THETA ** (-jnp.arange(half, dtype=jnp.float32) / half)
    ang = pos.astype(jnp.float32)[:, None] * inv[None, :]
    shape = (x.shape[0],) + (1,) * (x.ndim - 2) + (half,)
    cos = jnp.cos(ang).reshape(shape)
    sin = jnp.sin(ang).reshape(shape)
    xf = x.astype(jnp.float32)
    x1, x2 = xf[..., :half], xf[..., half:]
    return jnp.concatenate([x1 * cos - x2 * sin, x1 * sin + x2 * cos], axis=-1).astype(x.dtype)


def _mla_prefill(q_nope, q_rope, k_nope, k_rope, v):
    b, s, h, _ = q_nope.shape
    nb = s // Q_BLOCK
    qn = q_nope.reshape(b, nb, Q_BLOCK, h, QK_NOPE).swapaxes(0, 1)
    qr = q_rope.reshape(b, nb, Q_BLOCK, h, QK_ROPE).swapaxes(0, 1)
    k_pos = jnp.arange(s)

    def block(args):
        blk, qn_b, qr_b = args
        sc = (jnp.einsum('bqhn,bkhn->bhqk', qn_b, k_nope)
              + jnp.einsum('bqhr,bkr->bhqk', qr_b, k_rope)).astype(jnp.float32) * SM_SCALE
        q_pos = blk * Q_BLOCK + jnp.arange(Q_BLOCK)
        sc = jnp.where(k_pos[None, :] <= q_pos[:, None], sc, -jnp.inf)
        pr = jax.nn.softmax(sc, axis=-1).astype(v.dtype)
        return jnp.einsum('bhqk,bkhv->bqhv', pr, v)

    out = lax.map(block, (jnp.arange(nb), qn, qr))
    return out.swapaxes(0, 1).reshape(b, s, h, V_HEAD)


def _mla_decode(q_lat, q_rope, ckv_new, kr_new, cache_ckv, cache_kr, page_table):
    t = q_lat.shape[1]
    sc = (jnp.einsum('bthc,buc->bhtu', q_lat, ckv_new)
          + jnp.einsum('bthr,bur->bhtu', q_rope, kr_new)).astype(jnp.float32) * SM_SCALE
    causal = jnp.arange(t)[None, :] <= jnp.arange(t)[:, None]
    sc = jnp.where(causal, sc, -jnp.inf)
    m = sc.max(-1)
    pe = jnp.exp(sc - m[..., None])
    l = pe.sum(-1)
    acc = jnp.einsum('bhtu,buc->bhtc', pe, ckv_new.astype(jnp.float32))

    def step(carry, pages):
        m, l, acc = carry
        ck = cache_ckv[pages]
        kr = cache_kr[pages]
        s = (jnp.einsum('bthc,bpc->bhtp', q_lat, ck)
             + jnp.einsum('bthr,bpr->bhtp', q_rope, kr)).astype(jnp.float32) * SM_SCALE
        m_new = jnp.maximum(m, s.max(-1))
        corr = jnp.exp(m - m_new)
        p = jnp.exp(s - m_new[..., None])
        l = l * corr + p.sum(-1)
        acc = acc * corr[..., None] + jnp.einsum('bhtp,bpc->bhtc', p, ck.astype(jnp.float32))
        return (m_new, l, acc), None

    (m, l, acc), _ = lax.scan(step, (m, l, acc), page_table.T)
    return acc / l[..., None]


def _dwconv(u_padded, w, b):
    c = u_padded.shape[-1]
    out = lax.conv_general_dilated(u_padded, w.astype(u_padded.dtype)[:, None, :], window_strides=(1,),
                                   padding='VALID', dimension_numbers=('NWC', 'WIO', 'NWC'),
                                   feature_group_count=c)
    return out + b.astype(out.dtype)


def _shared_expert(x, w_g, w_u, w_d):
    return (jax.nn.silu(x @ w_g) * (x @ w_u)) @ w_d


def _routed_moe(x, w_router, b_router, w_eg, w_eu, w_ed):
    n, d = x.shape
    scores = jax.nn.sigmoid((x @ w_router).astype(jnp.float32))
    sel = scores + b_router.astype(jnp.float32)
    grp = sel.reshape(n, N_GROUPS, N_EXPERTS // N_GROUPS)
    grp_score = lax.top_k(grp, 2)[0].sum(-1)
    _, grp_idx = lax.top_k(grp_score, TOPK_GROUPS)
    grp_keep = jax.nn.one_hot(grp_idx, N_GROUPS, dtype=jnp.float32).sum(1) > 0
    keep = jnp.repeat(grp_keep, N_EXPERTS // N_GROUPS, axis=-1)
    _, e_idx = lax.top_k(jnp.where(keep, sel, -jnp.inf), TOP_K)
    e_w = jnp.take_along_axis(scores, e_idx, axis=-1)
    e_w = e_w / e_w.sum(-1, keepdims=True) * ROUTED_SCALE
    a = n * TOP_K
    flat_e = e_idx.reshape(a).astype(jnp.int32)
    flat_tok = jnp.arange(a, dtype=jnp.int32) // TOP_K
    flat_w = e_w.reshape(a)
    order = jnp.argsort(flat_e)
    se = flat_e[order]
    counts = jnp.bincount(flat_e, length=N_EXPERTS).astype(jnp.int32)
    starts = jnp.cumsum(counts) - counts
    pcounts = (counts + EXPERT_BLOCK - 1) // EXPERT_BLOCK * EXPERT_BLOCK
    pends = jnp.cumsum(pcounts)
    pstarts = pends - pcounts
    dest = pstarts[se] + jnp.arange(a, dtype=jnp.int32) - starts[se]
    n_blocks = (a + N_EXPERTS * (EXPERT_BLOCK - 1) + EXPERT_BLOCK - 1) // EXPERT_BLOCK
    rows = n_blocks * EXPERT_BLOCK
    row_tok = jnp.zeros((rows,), jnp.int32).at[dest].set(flat_tok[order])
    row_w = jnp.zeros((rows,), jnp.float32).at[dest].set(flat_w[order])
    blk_start = jnp.arange(n_blocks, dtype=jnp.int32) * EXPERT_BLOCK
    blk_e = jnp.minimum(jnp.searchsorted(pends, blk_start, side='right'), N_EXPERTS - 1)

    def step(acc, blk):
        e, tok, wt = blk
        h = x[tok]
        y = (jax.nn.silu(h @ w_eg[e]) * (h @ w_eu[e])) @ w_ed[e]
        return acc.at[tok].add(y.astype(jnp.float32) * wt[:, None]), None

    acc, _ = lax.scan(step, jnp.zeros((n, d), jnp.float32),
                      (blk_e, row_tok.reshape(n_blocks, EXPERT_BLOCK), row_w.reshape(n_blocks, EXPERT_BLOCK)))
    return acc.astype(x.dtype)


def setup_inputs(seed: int = 0) -> dict:
    key = jax.random.key(seed)
    ks = jax.random.split(key, 64)
    cnt = [0]

    def nxt():
        k = ks[cnt[0]]
        cnt[0] += 1
        return k

    def nrm(shape, scale=1.0):
        return jax.random.normal(nxt(), shape, jnp.float32) * scale

    def gain(shape):
        return 1.0 + nrm(shape, 0.05)

    L = DEPTH
    n_pages = PAST_LEN // PAGE_SIZE
    n_phys = (5 * DEC_BATCH * n_pages + 3) // 4
    x_prompt = nrm((BATCH, SEQ, D_MODEL))
    x_sample = nrm((DEC_BATCH, DEC_SEQ, D_MODEL))
    cache_ckv = nrm((L, n_phys, PAGE_SIZE, KV_LORA))
    cache_krope = nrm((L, n_phys, PAGE_SIZE, QK_ROPE))
    state_conv = nrm((L, DEC_BATCH, CONV_W - 1, CONV_DIM), 0.5)
    page_table = jax.random.permutation(nxt(), n_phys)[:DEC_BATCH * n_pages].reshape(DEC_BATCH, n_pages).astype(jnp.int32)
    p_prompt = nrm((L, BATCH, SEQ, PLE_DIM))
    p_sample = nrm((L, DEC_BATCH, DEC_SEQ, PLE_DIM))
    return {
        'x_prompt': x_prompt,
        'x_sample': x_sample,
        'cache_ckv': cache_ckv,
        'cache_krope': cache_krope,
        'state_conv': state_conv,
        'page_table': page_table,
        'p_prompt': p_prompt,
        'p_sample': p_sample,
        'w_in': nrm((L, D_MODEL, IN_COLS), D_MODEL ** -0.5),
        'b_cg': nrm((L, CG_COLS), 0.02),
        'g_qn': gain((L, Q_LORA)),
        'w_uq': nrm((L, Q_LORA, N_HEADS * (QK_NOPE + QK_ROPE)), Q_LORA ** -0.5),
        'g_kvn': gain((L, KV_LORA)),
        'w_uk': nrm((L, KV_LORA, N_HEADS, QK_NOPE), KV_LORA ** -0.5),
        'w_uv': nrm((L, KV_LORA, N_HEADS, V_HEAD), BETA * KV_LORA ** -0.5),
        'w_pa': nrm((L, N_HEADS * V_HEAD, D_MODEL), (N_HEADS * V_HEAD) ** -0.5),
        'w_dw': nrm((L, CONV_W, CONV_DIM), CONV_W ** -0.5),
        'b_dw': nrm((L, CONV_DIM), 0.02),
        'g_cn': gain((L, CONV_DIM)),
        'b_cn': nrm((L, CONV_DIM), 0.02),
        'w_pb': nrm((L, CONV_DIM, D_MODEL), CONV_DIM ** -0.5),
        'w_o': nrm((L, D_MODEL, D_MODEL), BETA * D_MODEL ** -0.5),
        'g_ln1': gain((L, D_MODEL)),
        'b_ln1': nrm((L, D_MODEL), 0.02),
        'w_router': nrm((L, D_MODEL, N_EXPERTS), D_MODEL ** -0.5),
        'b_router': nrm((L, N_EXPERTS), 0.01),
        'w_eg': nrm((L, N_EXPERTS, D_MODEL, D_EXPERT), D_MODEL ** -0.5),
        'w_eu': nrm((L, N_EXPERTS, D_MODEL, D_EXPERT), D_MODEL ** -0.5),
        'w_ed': nrm((L, N_EXPERTS, D_EXPERT, D_MODEL), BETA * D_EXPERT ** -0.5),
        'w_sg': nrm((L, D_MODEL, D_SHARED), D_MODEL ** -0.5),
        'w_su': nrm((L, D_MODEL, D_SHARED), D_MODEL ** -0.5),
        'w_sd': nrm((L, D_SHARED, D_MODEL), BETA * D_SHARED ** -0.5),
        'g_ln2': gain((L, D_MODEL)),
        'b_ln2': nrm((L, D_MODEL), 0.02),
        'w_ple': nrm((L, PLE_DIM, D_MODEL), PLE_DIM ** -0.5),
        'w_pleg': nrm((L, D_MODEL, D_MODEL), D_MODEL ** -0.5),
    }


def reference(x_prompt, x_sample, cache_ckv, cache_krope, state_conv, page_table, p_prompt, p_sample,
              w_in, b_cg, g_qn, w_uq, g_kvn, w_uk, w_uv, w_pa, w_dw, b_dw, g_cn, b_cn, w_pb, w_o,
              g_ln1, b_ln1, w_router, b_router, w_eg, w_eu, w_ed, w_sg, w_su, w_sd, g_ln2, b_ln2,
              w_ple, w_pleg):
    b, s, d = x_prompt.shape
    bd, t, _ = x_sample.shape
    n_p = b * s
    n_s = bd * t
    past = page_table.shape[1] * cache_ckv.shape[2]
    pos = jnp.concatenate([jnp.tile(jnp.arange(s, dtype=jnp.int32), b),
                           past + jnp.tile(jnp.arange(t, dtype=jnp.int32), bd)])
    x = jnp.concatenate([x_prompt.reshape(n_p, d), x_sample.reshape(n_s, d)], axis=0)
    o1 = Q_LORA
    o2 = Q_LORA + KV_LORA
    o3 = Q_LORA + KV_LORA + QK_ROPE
    ckv_p_l, kr_p_l, conv_p_l, ckv_s_l, kr_s_l, conv_s_l = [], [], [], [], [], []
    for i in range(DEPTH):
        z = x @ w_in[i]
        c_q, c_kv, k_r, z_cg = jnp.split(z, [o1, o2, o3], axis=-1)
        z_cg = z_cg + b_cg[i]
        z_conv, z_gate = jnp.split(z_cg, [2 * CONV_DIM], axis=-1)
        q = (_rmsnorm(c_q, g_qn[i]) @ w_uq[i]).reshape(-1, N_HEADS, QK_NOPE + QK_ROPE)
        q_nope = q[..., :QK_NOPE]
        q_rope = _rope(q[..., QK_NOPE:], pos)
        c_kv = _rmsnorm(c_kv, g_kvn[i])
        k_r = _rope(k_r, pos)
        ckv_p = c_kv[:n_p].reshape(b, s, KV_LORA)
        kr_p = k_r[:n_p].reshape(b, s, QK_ROPE)
        k_nope_p = jnp.einsum('bsc,chn->bshn', ckv_p, w_uk[i])
        v_p = jnp.einsum('bsc,chv->bshv', ckv_p, w_uv[i])
        o_p = _mla_prefill(q_nope[:n_p].reshape(b, s, N_HEADS, QK_NOPE),
                           q_rope[:n_p].reshape(b, s, N_HEADS, QK_ROPE), k_nope_p, kr_p, v_p)
        ckv_s = c_kv[n_p:].reshape(bd, t, KV_LORA)
        kr_s = k_r[n_p:].reshape(bd, t, QK_ROPE)
        q_lat = jnp.einsum('bthn,chn->bthc', q_nope[n_p:].reshape(bd, t, N_HEADS, QK_NOPE), w_uk[i])
        o_lat = _mla_decode(q_lat, q_rope[n_p:].reshape(bd, t, N_HEADS, QK_ROPE), ckv_s, kr_s,
                            cache_ckv[i], cache_krope[i], page_table)
        o_s = jnp.einsum('bhtc,chv->bthv', o_lat.astype(x.dtype), w_uv[i])
        attn = jnp.concatenate([o_p.reshape(n_p, N_HEADS * V_HEAD), o_s.reshape(n_s, N_HEADS * V_HEAD)], axis=0)
        g_lin, g_sig = jnp.split(z_conv, 2, axis=-1)
        u = g_lin * jax.nn.sigmoid(g_sig)
        u_p = jnp.concatenate([jnp.zeros((b, CONV_W - 1, CONV_DIM), u.dtype), u[:n_p].reshape(b, s, CONV_DIM)], axis=1)
        u_s = jnp.concatenate([state_conv[i].astype(u.dtype), u[n_p:].reshape(bd, t, CONV_DIM)], axis=1)
        cv = jnp.concatenate([_dwconv(u_p, w_dw[i], b_dw[i]).reshape(n_p, CONV_DIM),
                              _dwconv(u_s, w_dw[i], b_dw[i]).reshape(n_s, CONV_DIM)], axis=0)
        conv_out = jax.nn.silu(_layernorm(cv, g_cn[i], b_cn[i])) @ w_pb[i]
        g_a, g_b = jnp.split(z_gate, 2, axis=-1)
        h = jax.nn.sigmoid(g_a) * (attn @ w_pa[i]) + jax.nn.sigmoid(g_b) * conv_out
        x1 = _layernorm(ALPHA * x + h @ w_o[i], g_ln1[i], b_ln1[i])
        ffn = _shared_expert(x1, w_sg[i], w_su[i], w_sd[i]) + _routed_moe(x1, w_router[i], b_router[i], w_eg[i], w_eu[i], w_ed[i])
        x2 = _layernorm(ALPHA * x1 + ffn, g_ln2[i], b_ln2[i])
        p_all = jnp.concatenate([p_prompt[i].reshape(n_p, PLE_DIM), p_sample[i].reshape(n_s, PLE_DIM)], axis=0)
        x = x2 + (p_all @ w_ple[i]) * jax.nn.sigmoid(x2 @ w_pleg[i])
        ckv_p_l.append(ckv_p)
        kr_p_l.append(kr_p)
        conv_p_l.append(u_p[:, -(CONV_W - 1):])
        ckv_s_l.append(ckv_s)
        kr_s_l.append(kr_s)
        conv_s_l.append(u_s[:, -(CONV_W - 1):])
    y_prompt = x[:n_p].reshape(b, s, d)
    y_sample = x[n_p:].reshape(bd, t, d)
    return (y_prompt, y_sample, jnp.stack(ckv_p_l), jnp.stack(kr_p_l), jnp.stack(conv_p_l),
            jnp.stack(ckv_s_l), jnp.stack(kr_s_l), jnp.stack(conv_s_l))
```

```python
import functools

import jax
import jax.numpy as jnp
from jax import lax
from jax.experimental import pallas as pl
from jax.experimental.pallas import tpu as pltpu

F32 = jnp.float32
BF16 = jnp.bfloat16

D_MODEL = 1024
N_HEADS = 8
QK_NOPE = 128
QK_ROPE = 64
V_HEAD = 128
Q_LORA = 384
KV_LORA = 256
ROPE_THETA = 10000.0
SM_SCALE = (QK_NOPE + QK_ROPE) ** -0.5
CONV_W = 31
N_EXPERTS = 256
TOP_K = 8
N_GROUPS = 8
TOPK_GROUPS = 4
GROUP_SIZE = N_EXPERTS // N_GROUPS
D_EXPERT = D_MODEL // 4
D_SHARED = D_MODEL // 4
ROUTED_SCALE = 2.5
PLE_DIM = 256
LN_EPS = 1e-5
RMS_EPS = 1e-6

LANES = 128
QK_PAD = 2 * LANES
HIST = 32
EXPERT_ROWS = 256
VMEM_LIMIT = 56 * 1024 * 1024

_NEG_INF = float("-inf")


def _cparams(*sem):
    return pltpu.CompilerParams(dimension_semantics=sem, vmem_limit_bytes=VMEM_LIMIT)


def _const_spec(shape):
    nd = len(shape)
    return pl.BlockSpec(shape, lambda *_: (0,) * nd)


def _dot(a, b):
    return jnp.dot(a, b, preferred_element_type=F32)


def _dot_t(a, b):
    return lax.dot_general(a, b, (((1,), (1,)), ((), ())), preferred_element_type=F32)


def _layernorm(x, g, b):
    mu = jnp.mean(x, axis=-1, keepdims=True)
    xc = x - mu
    var = jnp.mean(xc * xc, axis=-1, keepdims=True)
    return xc * lax.rsqrt(var + LN_EPS) * g + b


def _rmsnorm(x, g):
    return x * lax.rsqrt(jnp.mean(x * x, axis=-1, keepdims=True) + RMS_EPS) * g


def _sigmoid(x):
    return 1.0 / (1.0 + jnp.exp(-x))


def _silu(x):
    return x * _sigmoid(x)


def _attn_proj_kernel(x_ref, cos_ref, sin_ref, wq_ref, wkv_ref, wkr_ref, gq_ref, gkv_ref,
                      wuq_ref, wuqs_ref, wuk_ref, wuv_ref,
                      ckv_ref, kr_ref, q_ref, k_ref, v_ref):
    xb = x_ref[...].astype(BF16)
    cos = cos_ref[...]
    sin = sin_ref[...]
    cq = _rmsnorm(_dot(xb, wq_ref[...]), gq_ref[...]).astype(BF16)
    ckv = _rmsnorm(_dot(xb, wkv_ref[...]), gkv_ref[...])
    ckv_ref[...] = ckv
    ckv_b = ckv.astype(BF16)
    zkr = _dot(xb, wkr_ref[...])
    kr = zkr[:, :LANES] * cos + zkr[:, LANES:] * sin
    kr_ref[...] = kr[:, :QK_ROPE]
    kr_b = kr.astype(BF16)
    qa = _dot(cq, wuq_ref[...])
    qs = _dot(cq, wuqs_ref[...])
    kn = _dot(ckv_b, wuk_ref[...])
    v_ref[...] = _dot(ckv_b, wuv_ref[...]).astype(BF16)
    for h in range(N_HEADS):
        c0 = h * QK_PAD
        q_ref[:, c0:c0 + LANES] = qa[:, c0:c0 + LANES].astype(BF16)
        q_ref[:, c0 + LANES:c0 + QK_PAD] = (
            qa[:, c0 + LANES:c0 + QK_PAD] * cos + qs[:, h * LANES:(h + 1) * LANES] * sin).astype(BF16)
        k_ref[:, c0:c0 + LANES] = kn[:, h * LANES:(h + 1) * LANES].astype(BF16)
        k_ref[:, c0 + LANES:c0 + QK_PAD] = kr_b


def _attn_proj(x, cos, sin, w, tm):
    n = x.shape[0]
    n_pos = cos.shape[0] // tm
    row = lambda i: (i, 0)
    pos_row = lambda i: (i % n_pos, 0)
    weights = (w["wq"], w["wkv"], w["wkr"], w["g_qn"], w["g_kvn"], w["wuq"], w["wuqs"], w["wuk"], w["wuv"])
    return pl.pallas_call(
        _attn_proj_kernel,
        grid=(n // tm,),
        in_specs=[pl.BlockSpec((tm, D_MODEL), row),
                  pl.BlockSpec((tm, LANES), pos_row),
                  pl.BlockSpec((tm, LANES), pos_row)] + [_const_spec(a.shape) for a in weights],
        out_specs=[pl.BlockSpec((tm, KV_LORA), row),
                   pl.BlockSpec((tm, QK_ROPE), row),
                   pl.BlockSpec((tm, N_HEADS * QK_PAD), row),
                   pl.BlockSpec((tm, N_HEADS * QK_PAD), row),
                   pl.BlockSpec((tm, N_HEADS * V_HEAD), row)],
        out_shape=[jax.ShapeDtypeStruct((n, KV_LORA), F32),
                   jax.ShapeDtypeStruct((n, QK_ROPE), F32),
                   jax.ShapeDtypeStruct((n, N_HEADS * QK_PAD), BF16),
                   jax.ShapeDtypeStruct((n, N_HEADS * QK_PAD), BF16),
                   jax.ShapeDtypeStruct((n, N_HEADS * V_HEAD), BF16)],
        compiler_params=_cparams("arbitrary"),
        name="attn_proj",
    )(x, cos, sin, *weights)


def _prefill_kernel(q_ref, k_ref, v_ref, o_ref, m_ref, l_ref, acc_ref, *, tq):
    qi = pl.program_id(2)
    q = q_ref[...]
    m_ref[...] = jnp.full(m_ref.shape, _NEG_INF, F32)
    l_ref[...] = jnp.zeros(l_ref.shape, F32)
    acc_ref[...] = jnp.zeros(acc_ref.shape, F32)

    def update(j, masked):
        start = pl.multiple_of(j * tq, tq)
        s = _dot_t(q, k_ref[pl.ds(start, tq), :]) * SM_SCALE
        if masked:
            r = lax.broadcasted_iota(jnp.int32, (tq, tq), 0)
            c = lax.broadcasted_iota(jnp.int32, (tq, tq), 1)
            s = jnp.where(c <= r, s, _NEG_INF)
        m_old = m_ref[...]
        m_new = jnp.maximum(m_old, jnp.max(s, axis=-1, keepdims=True))
        corr = jnp.exp(m_old - m_new)
        p = jnp.exp(s - m_new)
        l_ref[...] = l_ref[...] * corr + jnp.sum(p, axis=-1, keepdims=True)
        acc_ref[...] = acc_ref[...] * corr + _dot(p.astype(BF16), v_ref[pl.ds(start, tq), :])
        m_ref[...] = m_new

    def body(j, carry):
        update(j, False)
        return carry

    lax.fori_loop(0, qi, body, 0)
    update(qi, True)
    o_ref[...] = (acc_ref[...] / l_ref[...]).astype(o_ref.dtype)


def _prefill_attention(q, k, v, b, s, tq):
    q3 = q.reshape(b, s, N_HEADS * QK_PAD)
    k3 = k.reshape(b, s, N_HEADS * QK_PAD)
    v3 = v.reshape(b, s, N_HEADS * V_HEAD)
    out = pl.pallas_call(
        functools.partial(_prefill_kernel, tq=tq),
        grid=(b, N_HEADS, s // tq),
        in_specs=[pl.BlockSpec((None, tq, QK_PAD), lambda bi, h, i: (bi, i, h)),
                  pl.BlockSpec((None, s, QK_PAD), lambda bi, h, i: (bi, 0, h)),
                  pl.BlockSpec((None, s, V_HEAD), lambda bi, h, i: (bi, 0, h))],
        out_specs=pl.BlockSpec((None, tq, V_HEAD), lambda bi, h, i: (bi, i, h)),
        out_shape=jax.ShapeDtypeStruct((b, s, N_HEADS * V_HEAD), BF16),
        scratch_shapes=[pltpu.VMEM((tq, 1), F32), pltpu.VMEM((tq, 1), F32), pltpu.VMEM((tq, V_HEAD), F32)],
        compiler_params=_cparams("arbitrary", "arbitrary", "arbitrary"),
        name="prefill_attention",
    )(q3, k3, v3)
    return out.reshape(b * s, N_HEADS * V_HEAD)


def _head_matmul_kernel(a_ref, w_ref, o_ref):
    o_ref[...] = _dot(a_ref[...].astype(BF16), w_ref[...]).astype(o_ref.dtype)


def _head_matmul(a, w, col_stride, out_dtype):
    n = a.shape[0]
    _, kk, mm = w.shape
    return pl.pallas_call(
        _head_matmul_kernel,
        grid=(N_HEADS,),
        in_specs=[pl.BlockSpec((n, kk), lambda h: (0, h * col_stride)),
                  pl.BlockSpec((None, kk, mm), lambda h: (h, 0, 0))],
        out_specs=pl.BlockSpec((n, mm), lambda h: (0, h)),
        out_shape=jax.ShapeDtypeStruct((n, N_HEADS * mm), out_dtype),
        compiler_params=_cparams("arbitrary"),
        name="head_matmul",
    )(a, w)


def _decode_kernel(pt_ref, ql_ref, qr_ref, cn_ref, kn_ref, *rest, pages_per_step, t):
    del pt_ref
    pg = pages_per_step
    ck_refs = rest[:pg]
    kr_refs = rest[pg:2 * pg]
    o_ref, m_ref, l_ref, acc_ref = rest[2 * pg:]
    j = pl.program_id(1)
    ql = ql_ref[...]
    qr = qr_ref[...]
    rows = ql.shape[0]

    @pl.when(j == 0)
    def _():
        cn = cn_ref[...]
        cn_b = cn.astype(BF16)
        s = (_dot_t(ql, cn_b) + _dot_t(qr, kn_ref[...].astype(BF16))) * SM_SCALE
        ti = lax.broadcasted_iota(jnp.int32, (rows, t), 0) % t
        ui = lax.broadcasted_iota(jnp.int32, (rows, t), 1)
        s = jnp.where(ui <= ti, s, _NEG_INF)
        m = jnp.max(s, axis=-1, keepdims=True)
        p = jnp.exp(s - m)
        m_ref[...] = m
        l_ref[...] = jnp.sum(p, axis=-1, keepdims=True)
        acc_ref[...] = _dot(p.astype(BF16), cn_b)

    cks = [r[...].astype(BF16) for r in ck_refs]
    s = jnp.concatenate(
        [_dot_t(ql, ck) + _dot_t(qr, kr[...].astype(BF16)) for ck, kr in zip(cks, kr_refs)],
        axis=-1) * SM_SCALE
    m_old = m_ref[...]
    m_new = jnp.maximum(m_old, jnp.max(s, axis=-1, keepdims=True))
    corr = jnp.exp(m_old - m_new)
    p = jnp.exp(s - m_new)
    l_ref[...] = l_ref[...] * corr + jnp.sum(p, axis=-1, keepdims=True)
    pb = p.astype(BF16)
    page = cks[0].shape[0]
    pv = _dot(pb[:, :page], cks[0])
    for r in range(1, pg):
        pv = pv + _dot(pb[:, r * page:(r + 1) * page], cks[r])
    acc_ref[...] = acc_ref[...] * corr + pv
    m_ref[...] = m_new

    @pl.when(j == pl.num_programs(1) - 1)
    def _():
        o_ref[...] = acc_ref[...] / l_ref[...]


def _decode_attention(q_lat, q_rope, ckv_new, kr_new, cache_ckv, cache_kr, page_table, pages_per_step):
    bd, rows, _ = q_lat.shape
    t = ckv_new.shape[1]
    n_pages = page_table.shape[1]
    page = cache_ckv.shape[1]
    pg = pages_per_step
    seq = lambda b, j, pt: (b, 0, 0)

    def page_map(r):
        return lambda b, j, pt: (pt[b, j * pg + r], 0, 0)

    grid_spec = pltpu.PrefetchScalarGridSpec(
        num_scalar_prefetch=1,
        grid=(bd, n_pages // pg),
        in_specs=[pl.BlockSpec((None, rows, KV_LORA), seq),
                  pl.BlockSpec((None, rows, QK_ROPE), seq),
                  pl.BlockSpec((None, t, KV_LORA), seq),
                  pl.BlockSpec((None, t, QK_ROPE), seq)]
                 + [pl.BlockSpec((None, page, KV_LORA), page_map(r)) for r in range(pg)]
                 + [pl.BlockSpec((None, page, QK_ROPE), page_map(r)) for r in range(pg)],
        out_specs=pl.BlockSpec((None, rows, KV_LORA), seq),
        scratch_shapes=[pltpu.VMEM((rows, 1), F32), pltpu.VMEM((rows, 1), F32),
                        pltpu.VMEM((rows, KV_LORA), F32)],
    )
    return pl.pallas_call(
        functools.partial(_decode_kernel, pages_per_step=pg, t=t),
        grid_spec=grid_spec,
        out_shape=jax.ShapeDtypeStruct((bd, rows, KV_LORA), F32),
        compiler_params=_cparams("arbitrary", "arbitrary"),
        name="decode_attention",
    )(page_table, q_lat, q_rope, ckv_new, kr_new, *([cache_ckv] * pg), *([cache_kr] * pg))


def _conv_tail(cv, zga, zgb, gcn_ref, bcn_ref, wpb_ref, ga_ref, hb_ref):
    act = _silu(_layernorm(cv, gcn_ref[...], bcn_ref[...])).astype(BF16)
    hb_ref[...] = _sigmoid(zgb) * _dot(act, wpb_ref[...])
    ga_ref[...] = _sigmoid(zga)


def _conv_prompt_kernel(x_ref, wcg_ref, bcg_ref, wdw_ref, bdw_ref, gcn_ref, bcn_ref, wpb_ref,
                        ga_ref, hb_ref, ulast_ref, ubuf_ref, cv_ref, *, tm, chunk):
    i = pl.program_id(1)

    @pl.when(i == 0)
    def _():
        ubuf_ref[0:HIST, :] = jnp.zeros((HIST, D_MODEL), F32)

    z = _dot(x_ref[...].astype(BF16), wcg_ref[...]) + bcg_ref[...]
    u = z[:, :D_MODEL] * _sigmoid(z[:, D_MODEL:2 * D_MODEL])
    ubuf_ref[HIST:HIST + tm, :] = u
    ulast_ref[...] = u[tm - HIST:tm, :]
    off = HIST - (CONV_W - 1)
    for c in range(tm // chunk):
        r0 = c * chunk
        acc = jnp.broadcast_to(bdw_ref[...], (chunk, D_MODEL))
        for jt in range(CONV_W):
            acc = acc + wdw_ref[jt:jt + 1, :] * ubuf_ref[r0 + off + jt:r0 + off + jt + chunk, :]
        cv_ref[r0:r0 + chunk, :] = acc
    ubuf_ref[0:HIST, :] = ubuf_ref[tm:tm + HIST, :]
    _conv_tail(cv_ref[...], z[:, 2 * D_MODEL:3 * D_MODEL], z[:, 3 * D_MODEL:], gcn_ref, bcn_ref, wpb_ref,
               ga_ref, hb_ref)


def _conv_prompt(x3, w, tm):
    b, s, _ = x3.shape
    weights = (w["wcg"], w["b_cg"], w["w_dw"], w["b_dw"], w["g_cn"], w["b_cn"], w["wpb"])
    row = lambda bi, i: (bi, i, 0)
    return pl.pallas_call(
        functools.partial(_conv_prompt_kernel, tm=tm, chunk=32),
        grid=(b, s // tm),
        in_specs=[pl.BlockSpec((None, tm, D_MODEL), row)] + [_const_spec(a.shape) for a in weights],
        out_specs=[pl.BlockSpec((None, tm, D_MODEL), row),
                   pl.BlockSpec((None, tm, D_MODEL), row),
                   pl.BlockSpec((None, HIST, D_MODEL), lambda bi, i: (bi, 0, 0))],
        out_shape=[jax.ShapeDtypeStruct((b, s, D_MODEL), F32),
                   jax.ShapeDtypeStruct((b, s, D_MODEL), F32),
                   jax.ShapeDtypeStruct((b, HIST, D_MODEL), F32)],
        scratch_shapes=[pltpu.VMEM((HIST + tm, D_MODEL), F32), pltpu.VMEM((tm, D_MODEL), F32)],
        compiler_params=_cparams("arbitrary", "arbitrary"),
        name="conv_prompt",
    )(x3, *weights)


def _conv_sample_kernel(x_ref, st_ref, wcg_ref, bcg_ref, wdw_ref, bdw_ref, gcn_ref, bcn_ref, wpb_ref,
                        ga_ref, hb_ref, u_ref, *, t):
    zs = []
    for ti in range(t):
        z = _dot(x_ref[ti].astype(BF16), wcg_ref[...]) + bcg_ref[...]
        zs.append(z)
        u_ref[ti] = z[:, :D_MODEL] * _sigmoid(z[:, D_MODEL:2 * D_MODEL])
    hist = CONV_W - 1
    for ti in range(t):
        acc = jnp.broadcast_to(bdw_ref[...], u_ref.shape[1:])
        for jt in range(CONV_W):
            p = ti + jt
            src = st_ref[p] if p < hist else u_ref[p - hist]
            acc = acc + wdw_ref[jt:jt + 1, :] * src
        z = zs[ti]
        _conv_tail(acc, z[:, 2 * D_MODEL:3 * D_MODEL], z[:, 3 * D_MODEL:], gcn_ref, bcn_ref, wpb_ref,
                   ga_ref.at[ti], hb_ref.at[ti])


def _conv_sample(x_t, state_t, w, tb):
    t, bd, _ = x_t.shape
    hist = state_t.shape[0]
    weights = (w["wcg"], w["b_cg"], w["w_dw"], w["b_dw"], w["g_cn"], w["b_cn"], w["wpb"])
    blk = lambda i: (0, i, 0)
    return pl.pallas_call(
        functools.partial(_conv_sample_kernel, t=t),
        grid=(bd // tb,),
        in_specs=[pl.BlockSpec((t, tb, D_MODEL), blk), pl.BlockSpec((hist, tb, D_MODEL), blk)]
                 + [_const_spec(a.shape) for a in weights],
        out_specs=[pl.BlockSpec((t, tb, D_MODEL), blk)] * 3,
        out_shape=[jax.ShapeDtypeStruct((t, bd, D_MODEL), F32)] * 3,
        compiler_params=_cparams("arbitrary"),
        name="conv_sample",
    )(x_t, state_t, *weights)


def _first_argmax(vals, lane):
    m = jnp.max(vals, axis=-1, keepdims=True)
    idx = jnp.min(jnp.where(vals == m, lane, float(vals.shape[-1])), axis=-1, keepdims=True)
    return m, idx


def _route(scores, bias, eidx_ref, ew_ref):
    tm = scores.shape[0]
    lane = lax.broadcasted_iota(jnp.int32, (tm, N_EXPERTS), 1).astype(F32)
    group = jnp.floor(lane * (1.0 / GROUP_SIZE))
    sel = scores + bias
    gscore = []
    for g in range(N_GROUPS):
        vals = jnp.where(group == g, sel, _NEG_INF)
        m1, i1 = _first_argmax(vals, lane)
        m2 = jnp.max(jnp.where(lane == i1, _NEG_INF, vals), axis=-1, keepdims=True)
        gscore.append(m1 + m2)
    vals = jnp.full((tm, N_EXPERTS), _NEG_INF, F32)
    for g in range(N_GROUPS):
        beaten = jnp.zeros((tm, 1), F32)
        for o in range(N_GROUPS):
            if o != g:
                ahead = gscore[o] >= gscore[g] if o < g else gscore[o] > gscore[g]
                beaten = beaten + jnp.where(ahead, 1.0, 0.0)
        kept = jnp.where(beaten < TOPK_GROUPS, 1.0, 0.0)
        vals = jnp.where((group == g) & (kept > 0.5), sel, vals)
    out_lane = lax.broadcasted_iota(jnp.int32, (tm, TOP_K), 1)
    eidx = jnp.zeros((tm, TOP_K), F32)
    ew = jnp.zeros((tm, TOP_K), F32)
    for k in range(TOP_K):
        _, ik = _first_argmax(vals, lane)
        hit = lane == ik
        wk = jnp.sum(jnp.where(hit, scores, 0.0), axis=-1, keepdims=True)
        vals = jnp.where(hit, _NEG_INF, vals)
        eidx = jnp.where(out_lane == k, ik, eidx)
        ew = jnp.where(out_lane == k, wk, ew)
    eidx_ref[...] = eidx.astype(jnp.int32)
    ew_ref[...] = ew / jnp.sum(ew, axis=-1, keepdims=True) * ROUTED_SCALE


def _merge_kernel(attn_ref, ga_ref, hb_ref, x_ref, wpa_ref, wo_ref, g1_ref, b1_ref, wsgu_ref, wsd_ref,
                  wr_ref, br_ref, x1_ref, base_ref, eidx_ref, ew_ref, *, alpha):
    h = ga_ref[...] * _dot(attn_ref[...], wpa_ref[...]) + hb_ref[...]
    x1 = _layernorm(alpha * x_ref[...] + _dot(h.astype(BF16), wo_ref[...]), g1_ref[...], b1_ref[...])
    x1_ref[...] = x1
    x1b = x1.astype(BF16)
    gu = _dot(x1b, wsgu_ref[...])
    mid = (_silu(gu[:, :D_SHARED]) * gu[:, D_SHARED:]).astype(BF16)
    base_ref[...] = alpha * x1 + _dot(mid, wsd_ref[...])
    scores = _sigmoid(_dot(x1b, wr_ref[...]))
    _route(scores, br_ref[...], eidx_ref, ew_ref)


def _merge(attn, ga, hb, x, w, alpha, tm):
    n = x.shape[0]
    row = lambda i: (i, 0)
    weights = (w["wpa"], w["wo"], w["g_ln1"], w["b_ln1"], w["wsgu"], w["wsd"], w["wr"], w["b_router"])
    big = pl.BlockSpec((tm, D_MODEL), row)
    small = pl.BlockSpec((tm, TOP_K), row)
    return pl.pallas_call(
        functools.partial(_merge_kernel, alpha=alpha),
        grid=(n // tm,),
        in_specs=[big, big, big, big] + [_const_spec(a.shape) for a in weights],
        out_specs=[big, big, small, small],
        out_shape=[jax.ShapeDtypeStruct((n, D_MODEL), F32), jax.ShapeDtypeStruct((n, D_MODEL), F32),
                   jax.ShapeDtypeStruct((n, TOP_K), jnp.int32), jax.ShapeDtypeStruct((n, TOP_K), F32)],
        compiler_params=_cparams("arbitrary"),
        name="merge_route",
    )(attn, ga, hb, x, *weights)


def _onehots(eidx, tm):
    lane = lax.broadcasted_iota(jnp.int32, (tm, N_EXPERTS), 1)
    return [lane == eidx[:, k:k + 1] for k in range(TOP_K)]


def _rank_kernel(eidx_ref, rank_ref, count_ref, carry_ref, *, tm):
    @pl.when(pl.program_id(0) == 0)
    def _():
        carry_ref[...] = jnp.zeros(carry_ref.shape, F32)

    hits = _onehots(eidx_ref[...], tm)
    member = hits[0]
    for hk in hits[1:]:
        member = member | hk
    member_b = jnp.where(member, 1.0, 0.0).astype(BF16)
    r = lax.broadcasted_iota(jnp.int32, (tm, tm), 0)
    c = lax.broadcasted_iota(jnp.int32, (tm, tm), 1)
    below = jnp.where(c < r, 1.0, 0.0).astype(BF16)
    before = _dot(below, member_b) + carry_ref[...]
    out_lane = lax.broadcasted_iota(jnp.int32, (tm, TOP_K), 1)
    rank = jnp.zeros((tm, TOP_K), F32)
    for k in range(TOP_K):
        rk = jnp.sum(jnp.where(hits[k], before, 0.0), axis=-1, keepdims=True)
        rank = jnp.where(out_lane == k, rk, rank)
    rank_ref[...] = rank.astype(jnp.int32)
    total = carry_ref[...] + jnp.sum(member_b.astype(F32), axis=0, keepdims=True)
    carry_ref[...] = total
    count_ref[...] = total.astype(jnp.int32)


def _ranks(eidx, tm):
    n = eidx.shape[0]
    return pl.pallas_call(
        functools.partial(_rank_kernel, tm=tm),
        grid=(n // tm,),
        in_specs=[pl.BlockSpec((tm, TOP_K), lambda i: (i, 0))],
        out_specs=[pl.BlockSpec((tm, TOP_K), lambda i: (i, 0)), _const_spec((1, N_EXPERTS))],
        out_shape=[jax.ShapeDtypeStruct((n, TOP_K), jnp.int32),
                   jax.ShapeDtypeStruct((1, N_EXPERTS), jnp.int32)],
        scratch_shapes=[pltpu.VMEM((1, N_EXPERTS), F32)],
        compiler_params=_cparams("arbitrary"),
        name="expert_ranks",
    )(eidx)


def _dest_kernel(eidx_ref, rank_ref, pstart_ref, dest_ref, *, tm):
    hits = _onehots(eidx_ref[...], tm)
    pstart = pstart_ref[...]
    out_lane = lax.broadcasted_iota(jnp.int32, (tm, TOP_K), 1)
    base = jnp.zeros((tm, TOP_K), F32)
    for k in range(TOP_K):
        bk = jnp.sum(jnp.where(hits[k], pstart, 0.0), axis=-1, keepdims=True)
        base = jnp.where(out_lane == k, bk, base)
    dest_ref[...] = base.astype(jnp.int32) + rank_ref[...]


def _destinations(eidx, rank, pstart, tm):
    n = eidx.shape[0]
    row = pl.BlockSpec((tm, TOP_K), lambda i: (i, 0))
    return pl.pallas_call(
        functools.partial(_dest_kernel, tm=tm),
        grid=(n // tm,),
        in_specs=[row, row, _const_spec((1, N_EXPERTS))],
        out_specs=row,
        out_shape=jax.ShapeDtypeStruct((n, TOP_K), jnp.int32),
        compiler_params=_cparams("arbitrary"),
        name="expert_dest",
    )(eidx, rank, pstart)


def _row_copy(src_ref, src_row, dst_ref, dst_row, sem):
    return pltpu.make_async_copy(src_ref.at[pl.ds(src_row, 1), :], dst_ref.at[pl.ds(dst_row, 1), :], sem)


def _dispatch_kernel(dest_hbm, x_ref, zero_hbm, xs_hbm, dest_smem, isem, sem, *, tm):
    del zero_hbm
    i = pl.program_id(0)
    idx_copy = pltpu.make_async_copy(dest_hbm.at[i], dest_smem, isem)
    idx_copy.start()
    idx_copy.wait()

    def issue(r, carry):
        for k in range(TOP_K):
            _row_copy(x_ref, r, xs_hbm, dest_smem[r * TOP_K + k], sem).start()
        return carry

    lax.fori_loop(0, tm, issue, 0)

    def drain(r, carry):
        for k in range(TOP_K):
            _row_copy(x_ref, 0, xs_hbm, 0, sem).wait()
        return carry

    lax.fori_loop(0, tm, drain, 0)


def _dispatch(dest, x1, rows, tm):
    n = x1.shape[0]
    dest2 = dest.reshape(n // tm, tm * TOP_K)
    zeros = jnp.zeros((rows, D_MODEL), F32)
    return pl.pallas_call(
        functools.partial(_dispatch_kernel, tm=tm),
        grid=(n // tm,),
        in_specs=[pl.BlockSpec(memory_space=pl.ANY),
                  pl.BlockSpec((tm, D_MODEL), lambda i: (i, 0)),
                  pl.BlockSpec(memory_space=pl.ANY)],
        out_specs=pl.BlockSpec(memory_space=pl.ANY),
        out_shape=jax.ShapeDtypeStruct((rows, D_MODEL), F32),
        scratch_shapes=[pltpu.SMEM((tm * TOP_K,), jnp.int32), pltpu.SemaphoreType.DMA,
                        pltpu.SemaphoreType.DMA],
        input_output_aliases={2: 0},
        compiler_params=_cparams("arbitrary"),
        name="dispatch",
    )(dest2, x1, zeros)


def _expert_kernel(blk_e_ref, nused_ref, xs_ref, wg_ref, wu_ref, wd_ref, y_ref):
    del blk_e_ref
    g = pl.program_id(0)

    @pl.when(g < nused_ref[0])
    def _():
        xb = xs_ref[...].astype(BF16)
        gate = _dot(xb, wg_ref[...].astype(BF16))
        up = _dot(xb, wu_ref[...].astype(BF16))
        mid = (_silu(gate) * up).astype(BF16)
        y_ref[...] = _dot(mid, wd_ref[...].astype(BF16))

    @pl.when(g >= nused_ref[0])
    def _():
        y_ref[...] = jnp.zeros(y_ref.shape, F32)


def _expert_ffn(blk_e, nused, xs, w_eg, w_eu, w_ed):
    rows = xs.shape[0]
    n_blocks = rows // EXPERT_ROWS
    row = lambda g, be, nu: (g, 0)
    wsel = lambda g, be, nu: (be[g], 0, 0)
    grid_spec = pltpu.PrefetchScalarGridSpec(
        num_scalar_prefetch=2,
        grid=(n_blocks,),
        in_specs=[pl.BlockSpec((EXPERT_ROWS, D_MODEL), row),
                  pl.BlockSpec((None, D_MODEL, D_EXPERT), wsel),
                  pl.BlockSpec((None, D_MODEL, D_EXPERT), wsel),
                  pl.BlockSpec((None, D_EXPERT, D_MODEL), wsel)],
        out_specs=pl.BlockSpec((EXPERT_ROWS, D_MODEL), row),
    )
    return pl.pallas_call(
        _expert_kernel,
        grid_spec=grid_spec,
        out_shape=jax.ShapeDtypeStruct((rows, D_MODEL), F32),
        compiler_params=_cparams("arbitrary"),
        name="expert_ffn",
    )(blk_e, nused, xs, w_eg, w_eu, w_ed)


def _combine_kernel(dest_hbm, y_hbm, ew_ref, base_ref, p_ref, g2_ref, b2_ref, wple_ref, wpleg_ref,
                    out_ref, dest_smem, ybuf_ref, isem, sem, *, tm):
    i = pl.program_id(0)
    idx_copy = pltpu.make_async_copy(dest_hbm.at[i], dest_smem, isem)
    idx_copy.start()
    idx_copy.wait()

    def issue(r, carry):
        for k in range(TOP_K):
            _row_copy(y_hbm, dest_smem[r * TOP_K + k], ybuf_ref.at[k], r, sem).start()
        return carry

    lax.fori_loop(0, tm, issue, 0)

    def drain(r, carry):
        for k in range(TOP_K):
            _row_copy(y_hbm, 0, ybuf_ref.at[k], 0, sem).wait()
        return carry

    lax.fori_loop(0, tm, drain, 0)

    ew = ew_ref[...]
    routed = ew[:, 0:1] * ybuf_ref[0]
    for k in range(1, TOP_K):
        routed = routed + ew[:, k:k + 1] * ybuf_ref[k]
    x2 = _layernorm(base_ref[...] + routed, g2_ref[...], b2_ref[...])
    emb = _dot(p_ref[...].astype(BF16), wple_ref[...])
    out_ref[...] = x2 + emb * _sigmoid(_dot(x2.astype(BF16), wpleg_ref[...]))


def _combine(dest, y, ew, base, p, w, tm):
    n = base.shape[0]
    dest2 = dest.reshape(n // tm, tm * TOP_K)
    row = lambda i: (i, 0)
    weights = (w["g_ln2"], w["b_ln2"], w["wple"], w["wpleg"])
    return pl.pallas_call(
        functools.partial(_combine_kernel, tm=tm),
        grid=(n // tm,),
        in_specs=[pl.BlockSpec(memory_space=pl.ANY), pl.BlockSpec(memory_space=pl.ANY),
                  pl.BlockSpec((tm, TOP_K), row), pl.BlockSpec((tm, D_MODEL), row),
                  pl.BlockSpec((tm, PLE_DIM), row)] + [_const_spec(a.shape) for a in weights],
        out_specs=pl.BlockSpec((tm, D_MODEL), row),
        out_shape=jax.ShapeDtypeStruct((n, D_MODEL), F32),
        scratch_shapes=[pltpu.SMEM((tm * TOP_K,), jnp.int32), pltpu.VMEM((TOP_K, tm, D_MODEL), F32),
                        pltpu.SemaphoreType.DMA, pltpu.SemaphoreType.DMA],
        compiler_params=_cparams("arbitrary"),
        name="combine",
    )(dest2, y, ew, base, p, *weights)


def _rope_tables(pos):
    half = QK_ROPE // 2
    inv = ROPE_THETA ** (-jnp.arange(half, dtype=F32) / half)
    ang = pos.astype(F32)[:, None] * inv[None, :]
    reps = LANES // half
    return jnp.tile(jnp.cos(ang), (1, reps)), jnp.tile(jnp.sin(ang), (1, reps))


def _swap_halves(w):
    half = w.shape[-1] // 2
    return jnp.concatenate([-w[..., half:], w[..., :half]], axis=-1)


def _layer_weights(i, w_in, b_cg, g_qn, w_uq, g_kvn, w_uk, w_uv, w_pa, w_dw, b_dw, g_cn, b_cn, w_pb, w_o,
                   g_ln1, b_ln1, w_router, b_router, w_sg, w_su, w_sd, g_ln2, b_ln2, w_ple, w_pleg):
    o1, o2, o3 = Q_LORA, Q_LORA + KV_LORA, Q_LORA + KV_LORA + QK_ROPE
    win = w_in[i]
    row = lambda v: v[i].reshape(1, -1)
    wkr = win[:, o2:o3]
    zpad = jnp.zeros((D_MODEL, LANES - QK_ROPE), F32)
    uq = w_uq[i].reshape(Q_LORA, N_HEADS, QK_NOPE + QK_ROPE)
    uq_n, uq_r = uq[..., :QK_NOPE], uq[..., QK_NOPE:]
    hpad = jnp.zeros((Q_LORA, N_HEADS, LANES - QK_ROPE), F32)
    return {
        "wq": win[:, :o1].astype(BF16),
        "wkv": win[:, o1:o2].astype(BF16),
        "wkr": jnp.concatenate([wkr, zpad, _swap_halves(wkr), zpad], axis=1).astype(BF16),
        "wcg": win[:, o3:].astype(BF16),
        "b_cg": row(b_cg),
        "g_qn": row(g_qn),
        "g_kvn": row(g_kvn),
        "wuq": jnp.concatenate([uq_n, uq_r, hpad], axis=-1).reshape(Q_LORA, N_HEADS * QK_PAD).astype(BF16),
        "wuqs": jnp.concatenate([_swap_halves(uq_r), hpad], axis=-1).reshape(Q_LORA, N_HEADS * LANES).astype(BF16),
        "wuk": w_uk[i].reshape(KV_LORA, N_HEADS * QK_NOPE).astype(BF16),
        "wuv": w_uv[i].reshape(KV_LORA, N_HEADS * V_HEAD).astype(BF16),
        "wuk_t": w_uk[i].transpose(1, 2, 0).astype(BF16),
        "wuv_h": w_uv[i].transpose(1, 0, 2).astype(BF16),
        "wpa": w_pa[i].astype(BF16),
        "w_dw": w_dw[i],
        "b_dw": row(b_dw),
        "g_cn": row(g_cn),
        "b_cn": row(b_cn),
        "wpb": w_pb[i].astype(BF16),
        "wo": w_o[i].astype(BF16),
        "g_ln1": row(g_ln1),
        "b_ln1": row(b_ln1),
        "wr": w_router[i].astype(BF16),
        "b_router": row(b_router),
        "wsgu": jnp.concatenate([w_sg[i], w_su[i]], axis=1).astype(BF16),
        "wsd": w_sd[i].astype(BF16),
        "g_ln2": row(g_ln2),
        "b_ln2": row(b_ln2),
        "wple": w_ple[i].astype(BF16),
        "wpleg": w_pleg[i].astype(BF16),
    }


def _pick_tile(n, target):
    tm = min(n, target)
    while n % tm:
        tm //= 2
    return tm


def kernel(x_prompt, x_sample, cache_ckv, cache_krope, state_conv, page_table, p_prompt, p_sample, w_in, b_cg, g_qn, w_uq, g_kvn, w_uk, w_uv, w_pa, w_dw, b_dw, g_cn, b_cn, w_pb, w_o, g_ln1, b_ln1, w_router, b_router, w_eg, w_eu, w_ed, w_sg, w_su, w_sd, g_ln2, b_ln2, w_ple, w_pleg):
    b, s, d = x_prompt.shape
    bd, t, _ = x_sample.shape
    depth = w_in.shape[0]
    n_p, n_s = b * s, bd * t
    n = n_p + n_s
    past = page_table.shape[1] * cache_ckv.shape[2]
    alpha = (2 * depth) ** 0.25
    hist = CONV_W - 1

    cos_p, sin_p = _rope_tables(jnp.arange(s, dtype=jnp.int32))
    cos_s, sin_s = _rope_tables(past + jnp.tile(jnp.arange(t, dtype=jnp.int32), bd))

    tm_p = _pick_tile(s, 256)
    tm_s = _pick_tile(n_s, 256)
    tm_n = _pick_tile(n, 512)
    tm_g = _pick_tile(n, 128)
    tq = _pick_tile(s, 512)
    pages_per_step = _pick_tile(page_table.shape[1], 16)
    n_blocks = -(-(n * TOP_K + N_EXPERTS * (EXPERT_ROWS - 1)) // EXPERT_ROWS)

    xp = x_prompt.reshape(n_p, d)
    xs = x_sample.reshape(n_s, d)
    outs = {k: [] for k in ("ckv_p", "kr_p", "conv_p", "ckv_s", "kr_s", "conv_s")}
    for i in range(depth):
        w = _layer_weights(i, w_in, b_cg, g_qn, w_uq, g_kvn, w_uk, w_uv, w_pa, w_dw, b_dw, g_cn, b_cn, w_pb,
                           w_o, g_ln1, b_ln1, w_router, b_router, w_sg, w_su, w_sd, g_ln2, b_ln2, w_ple, w_pleg)
        ckv_p, kr_p, q_p, k_p, v_p = _attn_proj(xp, cos_p, sin_p, w, tm_p)
        attn_p = _prefill_attention(q_p, k_p, v_p, b, s, tq)
        ckv_s, kr_s, q_s, _, _ = _attn_proj(xs, cos_s, sin_s, w, tm_s)
        q_lat = _head_matmul(q_s, w["wuk_t"], 2, BF16)
        q_lat = q_lat.reshape(bd, t, N_HEADS, KV_LORA).transpose(0, 2, 1, 3).reshape(bd, N_HEADS * t, KV_LORA)
        q_r = q_s.reshape(bd, t, N_HEADS, QK_PAD)[..., QK_NOPE:QK_NOPE + QK_ROPE]
        q_r = q_r.transpose(0, 2, 1, 3).reshape(bd, N_HEADS * t, QK_ROPE)
        o_lat = _decode_attention(q_lat, q_r, ckv_s.reshape(bd, t, KV_LORA), kr_s.reshape(bd, t, QK_ROPE),
                                  cache_ckv[i], cache_krope[i], page_table, pages_per_step)
        o_lat = o_lat.reshape(bd, N_HEADS, t, KV_LORA).transpose(0, 2, 1, 3).reshape(n_s, N_HEADS * KV_LORA)
        attn_s = _head_matmul(o_lat, w["wuv_h"], 1, BF16)
        ga_p, hb_p, ulast = _conv_prompt(xp.reshape(b, s, d), w, tm_p)
        state_t = state_conv[i].transpose(1, 0, 2)
        ga_s, hb_s, u_s = _conv_sample(xs.reshape(bd, t, d).transpose(1, 0, 2), state_t, w, _pick_tile(bd, 128))
        from_t = lambda a: a.transpose(1, 0, 2).reshape(n_s, d)
        x_all = jnp.concatenate([xp, xs], axis=0)
        x1, base, eidx, ew = _merge(jnp.concatenate([attn_p, attn_s], axis=0),
                                    jnp.concatenate([ga_p.reshape(n_p, d), from_t(ga_s)], axis=0),
                                    jnp.concatenate([hb_p.reshape(n_p, d), from_t(hb_s)], axis=0),
                                    x_all, w, alpha, tm_n)
        rank, counts = _ranks(eidx, tm_n)
        pcounts = (counts[0] + EXPERT_ROWS - 1) // EXPERT_ROWS * EXPERT_ROWS
        pends = jnp.cumsum(pcounts)
        pstart = (pends - pcounts).astype(jnp.int32)
        nused = (pends[-1] // EXPERT_ROWS).astype(jnp.int32)
        blk = jnp.minimum(jnp.arange(n_blocks, dtype=jnp.int32), nused - 1) * EXPERT_ROWS
        blk_e = jnp.minimum(jnp.searchsorted(pends, blk, side="right"), N_EXPERTS - 1).astype(jnp.int32)
        dest = _destinations(eidx, rank, pstart.astype(F32).reshape(1, N_EXPERTS), tm_n)
        xsorted = _dispatch(dest, x1, n_blocks * EXPERT_ROWS, tm_g)
        y = _expert_ffn(blk_e, nused.reshape(1), xsorted, w_eg[i], w_eu[i], w_ed[i])
        p_all = jnp.concatenate([p_prompt[i].reshape(n_p, PLE_DIM), p_sample[i].reshape(n_s, PLE_DIM)], axis=0)
        x_all = _combine(dest, y, ew, base, p_all, w, tm_g)
        xp, xs = x_all[:n_p], x_all[n_p:]
        outs["ckv_p"].append(ckv_p.reshape(b, s, KV_LORA))
        outs["kr_p"].append(kr_p.reshape(b, s, QK_ROPE))
        outs["conv_p"].append(ulast[:, HIST - hist:, :])
        outs["ckv_s"].append(ckv_s.reshape(bd, t, KV_LORA))
        outs["kr_s"].append(kr_s.reshape(bd, t, QK_ROPE))
        outs["conv_s"].append(jnp.concatenate([state_conv[i], u_s.transpose(1, 0, 2)], axis=1)[:, -hist:])
    return (xp.reshape(b, s, d), xs.reshape(bd, t, d), jnp.stack(outs["ckv_p"]), jnp.stack(outs["kr_p"]),
            jnp.stack(outs["conv_p"]), jnp.stack(outs["ckv_s"]), jnp.stack(outs["kr_s"]),
            jnp.stack(outs["conv_s"]))
```

```python
import functools

import jax
import jax.numpy as jnp
from jax import lax
from jax.experimental import pallas as pl
from jax.experimental.pallas import tpu as pltpu

F32 = jnp.float32
BF16 = jnp.bfloat16

D_MODEL = 1024
N_HEADS = 8
QK_NOPE = 128
QK_ROPE = 64
V_HEAD = 128
Q_LORA = 384
KV_LORA = 256
ROPE_THETA = 10000.0
SM_SCALE = (QK_NOPE + QK_ROPE) ** -0.5
CONV_W = 31
N_EXPERTS = 256
TOP_K = 8
N_GROUPS = 8
TOPK_GROUPS = 4
GROUP_SIZE = N_EXPERTS // N_GROUPS
D_EXPERT = D_MODEL // 4
D_SHARED = D_MODEL // 4
ROUTED_SCALE = 2.5
PLE_DIM = 256
LN_EPS = 1e-5
RMS_EPS = 1e-6

LANES = 128
SUBLANES = 8
QK_PAD = 2 * LANES
V_PAD = 2 * LANES
HIST = 32
EXP2_SCALE = SM_SCALE * 1.4426950408889634
ATTN_ROW_CHUNK = 128
EXPERT_ROWS = 256
VMEM_LIMIT = 56 * 1024 * 1024

_NEG_INF = float("-inf")


def _cparams(*sem):
    return pltpu.CompilerParams(dimension_semantics=sem, vmem_limit_bytes=VMEM_LIMIT)


def _const_spec(shape):
    nd = len(shape)
    return pl.BlockSpec(shape, lambda *_: (0,) * nd)


def _dot(a, b):
    return jnp.dot(a, b, preferred_element_type=F32)


def _dot_t(a, b):
    return lax.dot_general(a, b, (((1,), (1,)), ((), ())), preferred_element_type=F32)


def _layernorm(x, g, b):
    mu = jnp.mean(x, axis=-1, keepdims=True)
    xc = x - mu
    var = jnp.mean(xc * xc, axis=-1, keepdims=True)
    return xc * lax.rsqrt(var + LN_EPS) * g + b


def _rmsnorm(x, g):
    return x * lax.rsqrt(jnp.mean(x * x, axis=-1, keepdims=True) + RMS_EPS) * g


def _sigmoid(x):
    return 1.0 / (1.0 + jnp.exp(-x))


def _silu(x):
    return x * _sigmoid(x)


def _attn_proj_kernel(x_ref, cos_ref, sin_ref, wq_ref, wkv_ref, wkr_ref, gq_ref, gkv_ref,
                      wuq_ref, wuqs_ref, wuk_ref, wuv_ref,
                      ckv_ref, kr_ref, q_ref, k_ref, v_ref):
    xb = x_ref[...].astype(BF16)
    cos = cos_ref[...]
    sin = sin_ref[...]
    cq = _rmsnorm(_dot(xb, wq_ref[...]), gq_ref[...]).astype(BF16)
    ckv = _rmsnorm(_dot(xb, wkv_ref[...]), gkv_ref[...])
    ckv_ref[...] = ckv
    ckv_b = ckv.astype(BF16)
    zkr = _dot(xb, wkr_ref[...])
    kr = zkr[:, :LANES] * cos + zkr[:, LANES:] * sin
    kr_ref[...] = kr[:, :QK_ROPE]
    kr_b = kr.astype(BF16)
    qa = _dot(cq, wuq_ref[...])
    qs = _dot(cq, wuqs_ref[...])
    kn = _dot(ckv_b, wuk_ref[...])
    vv = _dot(ckv_b, wuv_ref[...]).astype(BF16)
    ones = jnp.ones((vv.shape[0], V_PAD - V_HEAD), BF16)
    for h in range(N_HEADS):
        v_ref[:, h * V_PAD:h * V_PAD + V_HEAD] = vv[:, h * V_HEAD:(h + 1) * V_HEAD]
        v_ref[:, h * V_PAD + V_HEAD:(h + 1) * V_PAD] = ones
        c0 = h * QK_PAD
        q_ref[:, c0:c0 + LANES] = qa[:, c0:c0 + LANES].astype(BF16)
        q_ref[:, c0 + LANES:c0 + QK_PAD] = (
            qa[:, c0 + LANES:c0 + QK_PAD] * cos + qs[:, h * LANES:(h + 1) * LANES] * sin).astype(BF16)
        k_ref[:, c0:c0 + LANES] = kn[:, h * LANES:(h + 1) * LANES].astype(BF16)
        k_ref[:, c0 + LANES:c0 + QK_PAD] = kr_b


def _attn_proj(x, cos, sin, w, tm):
    n = x.shape[0]
    n_pos = cos.shape[0] // tm
    row = lambda i: (i, 0)
    pos_row = lambda i: (i % n_pos, 0)
    weights = (w["wq"], w["wkv"], w["wkr"], w["g_qn"], w["g_kvn"], w["wuq"], w["wuqs"], w["wuk"], w["wuv"])
    return pl.pallas_call(
        _attn_proj_kernel,
        grid=(n // tm,),
        in_specs=[pl.BlockSpec((tm, D_MODEL), row),
                  pl.BlockSpec((tm, LANES), pos_row),
                  pl.BlockSpec((tm, LANES), pos_row)] + [_const_spec(a.shape) for a in weights],
        out_specs=[pl.BlockSpec((tm, KV_LORA), row),
                   pl.BlockSpec((tm, QK_ROPE), row),
                   pl.BlockSpec((tm, N_HEADS * QK_PAD), row),
                   pl.BlockSpec((tm, N_HEADS * QK_PAD), row),
                   pl.BlockSpec((tm, N_HEADS * V_PAD), row)],
        out_shape=[jax.ShapeDtypeStruct((n, KV_LORA), F32),
                   jax.ShapeDtypeStruct((n, QK_ROPE), F32),
                   jax.ShapeDtypeStruct((n, N_HEADS * QK_PAD), BF16),
                   jax.ShapeDtypeStruct((n, N_HEADS * QK_PAD), BF16),
                   jax.ShapeDtypeStruct((n, N_HEADS * V_PAD), BF16)],
        compiler_params=_cparams("arbitrary"),
        name="attn_proj",
    )(x, cos, sin, *weights)


def _prefill_kernel(q_ref, k_ref, v_ref, o_ref, m_ref, acc_ref, s0_ref, s1_ref, p_ref, *, tq):
    qi = pl.program_id(2)
    q = q_ref[...]
    chunk = min(ATTN_ROW_CHUNK, tq)
    m_ref[...] = jnp.full(m_ref.shape, _NEG_INF, F32)
    acc_ref[...] = jnp.zeros(acc_ref.shape, F32)

    def scores(j, s_ref):
        start = pl.multiple_of(j * tq, tq)
        s_ref[...] = _dot_t(q, k_ref[pl.ds(start, tq), :])

    def consume(j, s_ref, masked):
        start = pl.multiple_of(j * tq, tq)
        for r0 in range(0, tq, chunk):
            rows = slice(r0, r0 + chunk)
            s = s_ref[rows, :] * EXP2_SCALE
            if masked:
                r = lax.broadcasted_iota(jnp.int32, (chunk, tq), 0) + r0
                c = lax.broadcasted_iota(jnp.int32, (chunk, tq), 1)
                s = jnp.where(c <= r, s, _NEG_INF)
            m_old = m_ref[rows, :]
            m_new = jnp.maximum(m_old, jnp.max(s, axis=-1, keepdims=True))
            p_ref[rows, :] = jnp.exp2(s - m_new).astype(BF16)
            acc_ref[rows, :] = acc_ref[rows, :] * jnp.exp2(m_old - m_new)
            m_ref[rows, :] = m_new
        acc_ref[...] += _dot(p_ref[...], v_ref[pl.ds(start, tq), :])

    scores(0, s0_ref)

    def body(i, carry):
        j = 2 * i
        scores(j + 1, s1_ref)
        consume(j, s0_ref, False)
        scores(j + 2, s0_ref)
        consume(j + 1, s1_ref, False)
        return carry

    lax.fori_loop(0, qi // 2, body, 0)

    @pl.when(qi % 2 == 0)
    def _():
        consume(qi, s0_ref, True)

    @pl.when(qi % 2 == 1)
    def _():
        scores(qi, s1_ref)
        consume(qi - 1, s0_ref, False)
        consume(qi, s1_ref, True)

    acc = acc_ref[...]
    o_ref[...] = (acc[:, :V_HEAD] / acc[:, V_HEAD:]).astype(o_ref.dtype)


def _prefill_attention(q, k, v, b, s, tq):
    q3 = q.reshape(b, s, N_HEADS * QK_PAD)
    k3 = k.reshape(b, s, N_HEADS * QK_PAD)
    v3 = v.reshape(b, s, N_HEADS * V_PAD)
    out = pl.pallas_call(
        functools.partial(_prefill_kernel, tq=tq),
        grid=(b, N_HEADS, s // tq),
        in_specs=[pl.BlockSpec((None, tq, QK_PAD), lambda bi, h, i: (bi, i, h)),
                  pl.BlockSpec((None, s, QK_PAD), lambda bi, h, i: (bi, 0, h)),
                  pl.BlockSpec((None, s, V_PAD), lambda bi, h, i: (bi, 0, h))],
        out_specs=pl.BlockSpec((None, tq, V_HEAD), lambda bi, h, i: (bi, i, h)),
        out_shape=jax.ShapeDtypeStruct((b, s, N_HEADS * V_HEAD), BF16),
        scratch_shapes=[pltpu.VMEM((tq, 1), F32), pltpu.VMEM((tq, V_PAD), F32),
                        pltpu.VMEM((tq, tq), F32), pltpu.VMEM((tq, tq), F32), pltpu.VMEM((tq, tq), BF16)],
        compiler_params=_cparams("arbitrary", "arbitrary", "arbitrary"),
        name="prefill_attention",
    )(q3, k3, v3)
    return out.reshape(b * s, N_HEADS * V_HEAD)


def _head_matmul_kernel(a_ref, w_ref, o_ref):
    o_ref[...] = _dot(a_ref[...].astype(BF16), w_ref[...]).astype(o_ref.dtype)


def _head_matmul(a, w, col_stride, out_dtype):
    n = a.shape[0]
    _, kk, mm = w.shape
    return pl.pallas_call(
        _head_matmul_kernel,
        grid=(N_HEADS,),
        in_specs=[pl.BlockSpec((n, kk), lambda h: (0, h * col_stride)),
                  pl.BlockSpec((None, kk, mm), lambda h: (h, 0, 0))],
        out_specs=pl.BlockSpec((n, mm), lambda h: (0, h)),
        out_shape=jax.ShapeDtypeStruct((n, N_HEADS * mm), out_dtype),
        compiler_params=_cparams("arbitrary"),
        name="head_matmul",
    )(a, w)


def _decode_kernel(pt_ref, ql_ref, qr_ref, cn_ref, kn_ref, *rest, pages_per_step, t):
    del pt_ref
    pg = pages_per_step
    ck_refs = rest[:pg]
    kr_refs = rest[pg:2 * pg]
    o_ref, m_ref, l_ref, acc_ref = rest[2 * pg:]
    j = pl.program_id(1)
    ql = ql_ref[...]
    qr = qr_ref[...]
    rows = ql.shape[0]

    @pl.when(j == 0)
    def _():
        cn = cn_ref[...]
        cn_b = cn.astype(BF16)
        s = (_dot_t(ql, cn_b) + _dot_t(qr, kn_ref[...].astype(BF16))) * SM_SCALE
        ti = lax.broadcasted_iota(jnp.int32, (rows, t), 0) % t
        ui = lax.broadcasted_iota(jnp.int32, (rows, t), 1)
        s = jnp.where(ui <= ti, s, _NEG_INF)
        m = jnp.max(s, axis=-1, keepdims=True)
        p = jnp.exp(s - m)
        m_ref[...] = m
        l_ref[...] = jnp.sum(p, axis=-1, keepdims=True)
        acc_ref[...] = _dot(p.astype(BF16), cn_b)

    cks = [r[...].astype(BF16) for r in ck_refs]
    s = jnp.concatenate(
        [_dot_t(ql, ck) + _dot(qr, kr[...].astype(BF16)) for ck, kr in zip(cks, kr_refs)],
        axis=-1) * SM_SCALE
    m_old = m_ref[...]
    m_new = jnp.maximum(m_old, jnp.max(s, axis=-1, keepdims=True))
    corr = jnp.exp(m_old - m_new)
    p = jnp.exp(s - m_new)
    l_ref[...] = l_ref[...] * corr + jnp.sum(p, axis=-1, keepdims=True)
    pb = p.astype(BF16)
    page = cks[0].shape[0]
    pv = _dot(pb[:, :page], cks[0])
    for r in range(1, pg):
        pv = pv + _dot(pb[:, r * page:(r + 1) * page], cks[r])
    acc_ref[...] = acc_ref[...] * corr + pv
    m_ref[...] = m_new

    @pl.when(j == pl.num_programs(1) - 1)
    def _():
        o_ref[...] = acc_ref[...] / l_ref[...]


def _decode_attention(q_lat, q_rope, ckv_new, kr_new, cache_ckv, cache_kr, page_table, pages_per_step):
    bd, rows, _ = q_lat.shape
    t = ckv_new.shape[1]
    n_pages = page_table.shape[1]
    page = cache_ckv.shape[1]
    pg = pages_per_step
    seq = lambda b, j, pt: (b, 0, 0)

    def page_map(r):
        return lambda b, j, pt: (pt[b, j * pg + r], 0, 0)

    grid_spec = pltpu.PrefetchScalarGridSpec(
        num_scalar_prefetch=1,
        grid=(bd, n_pages // pg),
        in_specs=[pl.BlockSpec((None, rows, KV_LORA), seq),
                  pl.BlockSpec((None, rows, QK_ROPE), seq),
                  pl.BlockSpec((None, t, KV_LORA), seq),
                  pl.BlockSpec((None, t, QK_ROPE), seq)]
                 + [pl.BlockSpec((None, page, KV_LORA), page_map(r)) for r in range(pg)]
                 + [pl.BlockSpec((None, QK_ROPE, page), page_map(r)) for r in range(pg)],
        out_specs=pl.BlockSpec((None, rows, KV_LORA), seq),
        scratch_shapes=[pltpu.VMEM((rows, 1), F32), pltpu.VMEM((rows, 1), F32),
                        pltpu.VMEM((rows, KV_LORA), F32)],
    )
    return pl.pallas_call(
        functools.partial(_decode_kernel, pages_per_step=pg, t=t),
        grid_spec=grid_spec,
        out_shape=jax.ShapeDtypeStruct((bd, rows, KV_LORA), F32),
        compiler_params=_cparams("arbitrary", "arbitrary"),
        name="decode_attention",
    )(page_table, q_lat, q_rope, ckv_new, kr_new, *([cache_ckv] * pg), *([cache_kr] * pg))


def _conv_tail(cv, zga, zgb, gcn_ref, bcn_ref, wpb_ref, ga_ref, hb_ref):
    act = _silu(_layernorm(cv, gcn_ref[...], bcn_ref[...])).astype(BF16)
    hb_ref[...] = _sigmoid(zgb) * _dot(act, wpb_ref[...])
    ga_ref[...] = _sigmoid(zga)


def _conv_prompt_kernel(x_ref, wcg_ref, bcg_ref, wdw_ref, bdw_ref, gcn_ref, bcn_ref, wpb_ref,
                        ga_ref, hb_ref, ulast_ref, ush_ref, cv_ref, *, tm, chunk):
    i = pl.program_id(1)
    rows = HIST + tm

    @pl.when(i == 0)
    def _():
        ush_ref[0, 0:HIST, :] = jnp.zeros((HIST, D_MODEL), F32)

    z = _dot(x_ref[...].astype(BF16), wcg_ref[...]) + bcg_ref[...]
    u = z[:, :D_MODEL] * _sigmoid(z[:, D_MODEL:2 * D_MODEL])
    ush_ref[0, HIST:rows, :] = u
    ulast_ref[...] = u[tm - HIST:tm, :]
    for p in range(1, SUBLANES):
        ush_ref[p, 0:rows - SUBLANES, :] = ush_ref[0, p:p + rows - SUBLANES, :]
    off = HIST - (CONV_W - 1)
    for c in range(tm // chunk):
        r0 = c * chunk
        acc = jnp.broadcast_to(bdw_ref[...], (chunk, D_MODEL))
        for jt in range(CONV_W):
            a, p = divmod(off + jt, SUBLANES)
            start = r0 + a * SUBLANES
            acc = acc + wdw_ref[jt:jt + 1, :] * ush_ref[p, start:start + chunk, :]
        cv_ref[r0:r0 + chunk, :] = acc
    ush_ref[0, 0:HIST, :] = ush_ref[0, tm:rows, :]
    _conv_tail(cv_ref[...], z[:, 2 * D_MODEL:3 * D_MODEL], z[:, 3 * D_MODEL:], gcn_ref, bcn_ref, wpb_ref,
               ga_ref, hb_ref)


def _conv_prompt(x3, w, tm):
    b, s, _ = x3.shape
    weights = (w["wcg"], w["b_cg"], w["w_dw"], w["b_dw"], w["g_cn"], w["b_cn"], w["wpb"])
    row = lambda bi, i: (bi, i, 0)
    return pl.pallas_call(
        functools.partial(_conv_prompt_kernel, tm=tm, chunk=32),
        grid=(b, s // tm),
        in_specs=[pl.BlockSpec((None, tm, D_MODEL), row)] + [_const_spec(a.shape) for a in weights],
        out_specs=[pl.BlockSpec((None, tm, D_MODEL), row),
                   pl.BlockSpec((None, tm, D_MODEL), row),
                   pl.BlockSpec((None, HIST, D_MODEL), lambda bi, i: (bi, 0, 0))],
        out_shape=[jax.ShapeDtypeStruct((b, s, D_MODEL), F32),
                   jax.ShapeDtypeStruct((b, s, D_MODEL), F32),
                   jax.ShapeDtypeStruct((b, HIST, D_MODEL), F32)],
        scratch_shapes=[pltpu.VMEM((SUBLANES, HIST + tm, D_MODEL), F32), pltpu.VMEM((tm, D_MODEL), F32)],
        compiler_params=_cparams("arbitrary", "arbitrary"),
        name="conv_prompt",
    )(x3, *weights)


def _conv_sample_kernel(x_ref, st_ref, wcg_ref, bcg_ref, wdw_ref, bdw_ref, gcn_ref, bcn_ref, wpb_ref,
                        ga_ref, hb_ref, u_ref, *, t):
    zs = []
    for ti in range(t):
        z = _dot(x_ref[ti].astype(BF16), wcg_ref[...]) + bcg_ref[...]
        zs.append(z)
        u_ref[ti] = z[:, :D_MODEL] * _sigmoid(z[:, D_MODEL:2 * D_MODEL])
    hist = CONV_W - 1
    for ti in range(t):
        acc = jnp.broadcast_to(bdw_ref[...], u_ref.shape[1:])
        for jt in range(CONV_W):
            p = ti + jt
            src = st_ref[p] if p < hist else u_ref[p - hist]
            acc = acc + wdw_ref[jt:jt + 1, :] * src
        z = zs[ti]
        _conv_tail(acc, z[:, 2 * D_MODEL:3 * D_MODEL], z[:, 3 * D_MODEL:], gcn_ref, bcn_ref, wpb_ref,
                   ga_ref.at[ti], hb_ref.at[ti])


def _conv_sample(x_t, state_t, w, tb):
    t, bd, _ = x_t.shape
    hist = state_t.shape[0]
    weights = (w["wcg"], w["b_cg"], w["w_dw"], w["b_dw"], w["g_cn"], w["b_cn"], w["wpb"])
    blk = lambda i: (0, i, 0)
    return pl.pallas_call(
        functools.partial(_conv_sample_kernel, t=t),
        grid=(bd // tb,),
        in_specs=[pl.BlockSpec((t, tb, D_MODEL), blk), pl.BlockSpec((hist, tb, D_MODEL), blk)]
                 + [_const_spec(a.shape) for a in weights],
        out_specs=[pl.BlockSpec((t, tb, D_MODEL), blk)] * 3,
        out_shape=[jax.ShapeDtypeStruct((t, bd, D_MODEL), F32)] * 3,
        compiler_params=_cparams("arbitrary"),
        name="conv_sample",
    )(x_t, state_t, *weights)


def _first_argmax(vals, lane):
    m = jnp.max(vals, axis=-1, keepdims=True)
    idx = jnp.min(jnp.where(vals == m, lane, float(vals.shape[-1])), axis=-1, keepdims=True)
    return m, idx


def _route(scores, bias, eidx_ref, ew_ref):
    tm = scores.shape[0]
    lane = lax.broadcasted_iota(jnp.int32, (tm, N_EXPERTS), 1).astype(F32)
    group = jnp.floor(lane * (1.0 / GROUP_SIZE))
    sel = scores + bias
    gscore = []
    for g in range(N_GROUPS):
        vals = jnp.where(group == g, sel, _NEG_INF)
        m1, i1 = _first_argmax(vals, lane)
        m2 = jnp.max(jnp.where(lane == i1, _NEG_INF, vals), axis=-1, keepdims=True)
        gscore.append(m1 + m2)
    vals = jnp.full((tm, N_EXPERTS), _NEG_INF, F32)
    for g in range(N_GROUPS):
        beaten = jnp.zeros((tm, 1), F32)
        for o in range(N_GROUPS):
            if o != g:
                ahead = gscore[o] >= gscore[g] if o < g else gscore[o] > gscore[g]
                beaten = beaten + jnp.where(ahead, 1.0, 0.0)
        kept = jnp.where(beaten < TOPK_GROUPS, 1.0, 0.0)
        vals = jnp.where((group == g) & (kept > 0.5), sel, vals)
    out_lane = lax.broadcasted_iota(jnp.int32, (tm, TOP_K), 1)
    eidx = jnp.zeros((tm, TOP_K), F32)
    ew = jnp.zeros((tm, TOP_K), F32)
    for k in range(TOP_K):
        _, ik = _first_argmax(vals, lane)
        hit = lane == ik
        wk = jnp.sum(jnp.where(hit, scores, 0.0), axis=-1, keepdims=True)
        vals = jnp.where(hit, _NEG_INF, vals)
        eidx = jnp.where(out_lane == k, ik, eidx)
        ew = jnp.where(out_lane == k, wk, ew)
    eidx_ref[...] = eidx.astype(jnp.int32)
    ew_ref[...] = ew / jnp.sum(ew, axis=-1, keepdims=True) * ROUTED_SCALE


def _merge_kernel(attn_ref, ga_ref, hb_ref, x_ref, wpa_ref, wo_ref, g1_ref, b1_ref, wsgu_ref, wsd_ref,
                  wr_ref, br_ref, x1_ref, base_ref, eidx_ref, ew_ref, *, alpha):
    h = ga_ref[...] * _dot(attn_ref[...], wpa_ref[...]) + hb_ref[...]
    x1 = _layernorm(alpha * x_ref[...] + _dot(h.astype(BF16), wo_ref[...]), g1_ref[...], b1_ref[...])
    x1_ref[...] = x1
    x1b = x1.astype(BF16)
    gu = _dot(x1b, wsgu_ref[...])
    mid = (_silu(gu[:, :D_SHARED]) * gu[:, D_SHARED:]).astype(BF16)
    base_ref[...] = alpha * x1 + _dot(mid, wsd_ref[...])
    scores = _sigmoid(_dot(x1b, wr_ref[...]))
    _route(scores, br_ref[...], eidx_ref, ew_ref)


def _merge(attn, ga, hb, x, w, alpha, tm):
    n = x.shape[0]
    row = lambda i: (i, 0)
    weights = (w["wpa"], w["wo"], w["g_ln1"], w["b_ln1"], w["wsgu"], w["wsd"], w["wr"], w["b_router"])
    big = pl.BlockSpec((tm, D_MODEL), row)
    small = pl.BlockSpec((tm, TOP_K), row)
    return pl.pallas_call(
        functools.partial(_merge_kernel, alpha=alpha),
        grid=(n // tm,),
        in_specs=[big, big, big, big] + [_const_spec(a.shape) for a in weights],
        out_specs=[big, big, small, small],
        out_shape=[jax.ShapeDtypeStruct((n, D_MODEL), F32), jax.ShapeDtypeStruct((n, D_MODEL), F32),
                   jax.ShapeDtypeStruct((n, TOP_K), jnp.int32), jax.ShapeDtypeStruct((n, TOP_K), F32)],
        compiler_params=_cparams("arbitrary"),
        name="merge_route",
    )(attn, ga, hb, x, *weights)


def _onehots(eidx, tm):
    lane = lax.broadcasted_iota(jnp.int32, (tm, N_EXPERTS), 1)
    return [lane == eidx[:, k:k + 1] for k in range(TOP_K)]


def _rank_kernel(eidx_ref, rank_ref, count_ref, carry_ref, *, tm):
    @pl.when(pl.program_id(0) == 0)
    def _():
        carry_ref[...] = jnp.zeros(carry_ref.shape, F32)

    hits = _onehots(eidx_ref[...], tm)
    member = hits[0]
    for hk in hits[1:]:
        member = member | hk
    member_b = jnp.where(member, 1.0, 0.0).astype(BF16)
    r = lax.broadcasted_iota(jnp.int32, (tm, tm), 0)
    c = lax.broadcasted_iota(jnp.int32, (tm, tm), 1)
    below = jnp.where(c < r, 1.0, 0.0).astype(BF16)
    before = _dot(below, member_b) + carry_ref[...]
    out_lane = lax.broadcasted_iota(jnp.int32, (tm, TOP_K), 1)
    rank = jnp.zeros((tm, TOP_K), F32)
    for k in range(TOP_K):
        rk = jnp.sum(jnp.where(hits[k], before, 0.0), axis=-1, keepdims=True)
        rank = jnp.where(out_lane == k, rk, rank)
    rank_ref[...] = rank.astype(jnp.int32)
    total = carry_ref[...] + jnp.sum(member_b.astype(F32), axis=0, keepdims=True)
    carry_ref[...] = total
    count_ref[...] = total.astype(jnp.int32)


def _ranks(eidx, tm):
    n = eidx.shape[0]
    return pl.pallas_call(
        functools.partial(_rank_kernel, tm=tm),
        grid=(n // tm,),
        in_specs=[pl.BlockSpec((tm, TOP_K), lambda i: (i, 0))],
        out_specs=[pl.BlockSpec((tm, TOP_K), lambda i: (i, 0)), _const_spec((1, N_EXPERTS))],
        out_shape=[jax.ShapeDtypeStruct((n, TOP_K), jnp.int32),
                   jax.ShapeDtypeStruct((1, N_EXPERTS), jnp.int32)],
        scratch_shapes=[pltpu.VMEM((1, N_EXPERTS), F32)],
        compiler_params=_cparams("arbitrary"),
        name="expert_ranks",
    )(eidx)


def _dest_kernel(eidx_ref, rank_ref, pstart_ref, dest_ref, *, tm):
    hits = _onehots(eidx_ref[...], tm)
    pstart = pstart_ref[...]
    out_lane = lax.broadcasted_iota(jnp.int32, (tm, TOP_K), 1)
    base = jnp.zeros((tm, TOP_K), F32)
    for k in range(TOP_K):
        bk = jnp.sum(jnp.where(hits[k], pstart, 0.0), axis=-1, keepdims=True)
        base = jnp.where(out_lane == k, bk, base)
    dest_ref[...] = base.astype(jnp.int32) + rank_ref[...]


def _destinations(eidx, rank, pstart, tm):
    n = eidx.shape[0]
    row = pl.BlockSpec((tm, TOP_K), lambda i: (i, 0))
    return pl.pallas_call(
        functools.partial(_dest_kernel, tm=tm),
        grid=(n // tm,),
        in_specs=[row, row, _const_spec((1, N_EXPERTS))],
        out_specs=row,
        out_shape=jax.ShapeDtypeStruct((n, TOP_K), jnp.int32),
        compiler_params=_cparams("arbitrary"),
        name="expert_dest",
    )(eidx, rank, pstart)


def _row_copy(src_ref, src_row, dst_ref, dst_row, sem):
    return pltpu.make_async_copy(src_ref.at[pl.ds(src_row, 1), :], dst_ref.at[pl.ds(dst_row, 1), :], sem)


_PAD_PIECES = tuple(EXPERT_ROWS >> (bit + 1) for bit in range(EXPERT_ROWS.bit_length() - 1)
                    if EXPERT_ROWS >> (bit + 1) >= SUBLANES)


def _zero_fill(pad_start_ref, pad_count_ref, zero_ref, xs_hbm, sem, wait):
    def go(cp):
        if wait:
            cp.wait()
        else:
            cp.start()

    def per_expert(e, carry):
        start = pad_start_ref[e]
        lead = (-start) & (SUBLANES - 1)
        for r in range(SUBLANES - 1):
            @pl.when(r < lead)
            def _(r=r):
                go(_row_copy(zero_ref, 0, xs_hbm, start + r, sem))

        rest = pad_count_ref[e] - lead
        pos = start + lead
        for size in _PAD_PIECES:
            has = (rest & size) != 0

            @pl.when(has)
            def _(pos=pos, size=size):
                dst = xs_hbm.at[pl.ds(pl.multiple_of(pos, SUBLANES), size), :]
                go(pltpu.make_async_copy(zero_ref.at[pl.ds(0, size), :], dst, sem))

            pos = pos + jnp.where(has, size, 0)
        return carry

    lax.fori_loop(0, N_EXPERTS, per_expert, 0)

    piece = zero_ref.shape[0]

    def tail(j, carry):
        dst = xs_hbm.at[pl.ds(pl.multiple_of(j * piece, piece), piece), :]
        go(pltpu.make_async_copy(zero_ref, dst, sem))
        return carry

    end = pad_start_ref[N_EXPERTS - 1] + pad_count_ref[N_EXPERTS - 1]
    lax.fori_loop(end // piece, xs_hbm.shape[0] // piece, tail, 0)


def _dispatch_kernel(pad_start_ref, pad_count_ref, dest_hbm, x_ref, xs_hbm, dest_smem, zero_ref, isem, sem,
                     zsem, *, tm):
    i = pl.program_id(0)

    @pl.when(i == 0)
    def _():
        zero_ref[...] = jnp.zeros(zero_ref.shape, F32)
        _zero_fill(pad_start_ref, pad_count_ref, zero_ref, xs_hbm, zsem, wait=False)

    idx_copy = pltpu.make_async_copy(dest_hbm.at[i], dest_smem, isem)
    idx_copy.start()
    idx_copy.wait()

    def issue(r, carry):
        for k in range(TOP_K):
            _row_copy(x_ref, r, xs_hbm, dest_smem[r * TOP_K + k], sem).start()
        return carry

    lax.fori_loop(0, tm, issue, 0)

    def drain(r, carry):
        for k in range(TOP_K):
            _row_copy(x_ref, 0, xs_hbm, 0, sem).wait()
        return carry

    lax.fori_loop(0, tm, drain, 0)

    @pl.when(i == 0)
    def _():
        _zero_fill(pad_start_ref, pad_count_ref, zero_ref, xs_hbm, zsem, wait=True)


def _dispatch(dest, x1, pad_start, pad_count, rows, tm):
    n = x1.shape[0]
    dest2 = dest.reshape(n // tm, tm * TOP_K)
    grid_spec = pltpu.PrefetchScalarGridSpec(
        num_scalar_prefetch=2,
        grid=(n // tm,),
        in_specs=[pl.BlockSpec(memory_space=pl.ANY),
                  pl.BlockSpec((tm, D_MODEL), lambda i, ps, pc: (i, 0))],
        out_specs=pl.BlockSpec(memory_space=pl.ANY),
        scratch_shapes=[pltpu.SMEM((tm * TOP_K,), jnp.int32), pltpu.VMEM((_PAD_PIECES[0], D_MODEL), F32),
                        pltpu.SemaphoreType.DMA, pltpu.SemaphoreType.DMA, pltpu.SemaphoreType.DMA],
    )
    return pl.pallas_call(
        functools.partial(_dispatch_kernel, tm=tm),
        grid_spec=grid_spec,
        out_shape=jax.ShapeDtypeStruct((rows, D_MODEL), F32),
        compiler_params=_cparams("arbitrary"),
        name="dispatch",
    )(pad_start, pad_count, dest2, x1)


def _expert_kernel(blk_e_ref, nused_ref, xs_ref, wg_ref, wu_ref, wd_ref, y_ref):
    del blk_e_ref
    g = pl.program_id(0)

    @pl.when(g < nused_ref[0])
    def _():
        xb = xs_ref[...].astype(BF16)
        gate = _dot(xb, wg_ref[...].astype(BF16))
        up = _dot(xb, wu_ref[...].astype(BF16))
        mid = (_silu(gate) * up).astype(BF16)
        y_ref[...] = _dot(mid, wd_ref[...].astype(BF16))

    @pl.when(g >= nused_ref[0])
    def _():
        y_ref[...] = jnp.zeros(y_ref.shape, F32)


def _expert_ffn(blk_e, nused, xs, w_eg, w_eu, w_ed):
    rows = xs.shape[0]
    n_blocks = rows // EXPERT_ROWS
    row = lambda g, be, nu: (g, 0)
    in_row = lambda g, be, nu: (jnp.minimum(g, nu[0] - 1), 0)
    wsel = lambda g, be, nu: (be[g], 0, 0)
    grid_spec = pltpu.PrefetchScalarGridSpec(
        num_scalar_prefetch=2,
        grid=(n_blocks,),
        in_specs=[pl.BlockSpec((EXPERT_ROWS, D_MODEL), in_row),
                  pl.BlockSpec((None, D_MODEL, D_EXPERT), wsel),
                  pl.BlockSpec((None, D_MODEL, D_EXPERT), wsel),
                  pl.BlockSpec((None, D_EXPERT, D_MODEL), wsel)],
        out_specs=pl.BlockSpec((EXPERT_ROWS, D_MODEL), row),
    )
    return pl.pallas_call(
        _expert_kernel,
        grid_spec=grid_spec,
        out_shape=jax.ShapeDtypeStruct((rows, D_MODEL), F32),
        compiler_params=_cparams("arbitrary"),
        name="expert_ffn",
    )(blk_e, nused, xs, w_eg, w_eu, w_ed)


def _combine_kernel(dest_hbm, y_hbm, ew_ref, base_ref, p_ref, g2_ref, b2_ref, wple_ref, wpleg_ref,
                    out_ref, dest_smem, ybuf_ref, isem, sem, *, tm):
    i = pl.program_id(0)
    idx_copy = pltpu.make_async_copy(dest_hbm.at[i], dest_smem, isem)
    idx_copy.start()
    idx_copy.wait()

    def issue(r, carry):
        for k in range(TOP_K):
            _row_copy(y_hbm, dest_smem[r * TOP_K + k], ybuf_ref.at[k], r, sem).start()
        return carry

    lax.fori_loop(0, tm, issue, 0)

    def drain(r, carry):
        for k in range(TOP_K):
            _row_copy(y_hbm, 0, ybuf_ref.at[k], 0, sem).wait()
        return carry

    lax.fori_loop(0, tm, drain, 0)

    ew = ew_ref[...]
    routed = ew[:, 0:1] * ybuf_ref[0]
    for k in range(1, TOP_K):
        routed = routed + ew[:, k:k + 1] * ybuf_ref[k]
    x2 = _layernorm(base_ref[...] + routed, g2_ref[...], b2_ref[...])
    emb = _dot(p_ref[...].astype(BF16), wple_ref[...])
    out_ref[...] = x2 + emb * _sigmoid(_dot(x2.astype(BF16), wpleg_ref[...]))


def _combine(dest, y, ew, base, p, w, tm):
    n = base.shape[0]
    dest2 = dest.reshape(n // tm, tm * TOP_K)
    row = lambda i: (i, 0)
    weights = (w["g_ln2"], w["b_ln2"], w["wple"], w["wpleg"])
    return pl.pallas_call(
        functools.partial(_combine_kernel, tm=tm),
        grid=(n // tm,),
        in_specs=[pl.BlockSpec(memory_space=pl.ANY), pl.BlockSpec(memory_space=pl.ANY),
                  pl.BlockSpec((tm, TOP_K), row), pl.BlockSpec((tm, D_MODEL), row),
                  pl.BlockSpec((tm, PLE_DIM), row)] + [_const_spec(a.shape) for a in weights],
        out_specs=pl.BlockSpec((tm, D_MODEL), row),
        out_shape=jax.ShapeDtypeStruct((n, D_MODEL), F32),
        scratch_shapes=[pltpu.SMEM((tm * TOP_K,), jnp.int32), pltpu.VMEM((TOP_K, tm, D_MODEL), F32),
                        pltpu.SemaphoreType.DMA, pltpu.SemaphoreType.DMA],
        compiler_params=_cparams("arbitrary"),
        name="combine",
    )(dest2, y, ew, base, p, *weights)


def _rope_tables(pos):
    half = QK_ROPE // 2
    inv = ROPE_THETA ** (-jnp.arange(half, dtype=F32) / half)
    ang = pos.astype(F32)[:, None] * inv[None, :]
    reps = LANES // half
    return jnp.tile(jnp.cos(ang), (1, reps)), jnp.tile(jnp.sin(ang), (1, reps))


def _swap_halves(w):
    half = w.shape[-1] // 2
    return jnp.concatenate([-w[..., half:], w[..., :half]], axis=-1)


def _layer_weights(i, w_in, b_cg, g_qn, w_uq, g_kvn, w_uk, w_uv, w_pa, w_dw, b_dw, g_cn, b_cn, w_pb, w_o,
                   g_ln1, b_ln1, w_router, b_router, w_sg, w_su, w_sd, g_ln2, b_ln2, w_ple, w_pleg):
    o1, o2, o3 = Q_LORA, Q_LORA + KV_LORA, Q_LORA + KV_LORA + QK_ROPE
    win = w_in[i]
    row = lambda v: v[i].reshape(1, -1)
    wkr = win[:, o2:o3]
    zpad = jnp.zeros((D_MODEL, LANES - QK_ROPE), F32)
    uq = w_uq[i].reshape(Q_LORA, N_HEADS, QK_NOPE + QK_ROPE)
    uq_n, uq_r = uq[..., :QK_NOPE], uq[..., QK_NOPE:]
    hpad = jnp.zeros((Q_LORA, N_HEADS, LANES - QK_ROPE), F32)
    return {
        "wq": win[:, :o1].astype(BF16),
        "wkv": win[:, o1:o2].astype(BF16),
        "wkr": jnp.concatenate([wkr, zpad, _swap_halves(wkr), zpad], axis=1).astype(BF16),
        "wcg": win[:, o3:].astype(BF16),
        "b_cg": row(b_cg),
        "g_qn": row(g_qn),
        "g_kvn": row(g_kvn),
        "wuq": jnp.concatenate([uq_n, uq_r, hpad], axis=-1).reshape(Q_LORA, N_HEADS * QK_PAD).astype(BF16),
        "wuqs": jnp.concatenate([_swap_halves(uq_r), hpad], axis=-1).reshape(Q_LORA, N_HEADS * LANES).astype(BF16),
        "wuk": w_uk[i].reshape(KV_LORA, N_HEADS * QK_NOPE).astype(BF16),
        "wuv": w_uv[i].reshape(KV_LORA, N_HEADS * V_HEAD).astype(BF16),
        "wuk_t": w_uk[i].transpose(1, 2, 0).astype(BF16),
        "wuv_h": w_uv[i].transpose(1, 0, 2).astype(BF16),
        "wpa": w_pa[i].astype(BF16),
        "w_dw": w_dw[i],
        "b_dw": row(b_dw),
        "g_cn": row(g_cn),
        "b_cn": row(b_cn),
        "wpb": w_pb[i].astype(BF16),
        "wo": w_o[i].astype(BF16),
        "g_ln1": row(g_ln1),
        "b_ln1": row(b_ln1),
        "wr": w_router[i].astype(BF16),
        "b_router": row(b_router),
        "wsgu": jnp.concatenate([w_sg[i], w_su[i]], axis=1).astype(BF16),
        "wsd": w_sd[i].astype(BF16),
        "g_ln2": row(g_ln2),
        "b_ln2": row(b_ln2),
        "wple": w_ple[i].astype(BF16),
        "wpleg": w_pleg[i].astype(BF16),
    }


def _pick_tile(n, target):
    tm = min(n, target)
    while n % tm:
        tm //= 2
    return tm


def kernel(x_prompt, x_sample, cache_ckv, cache_krope, state_conv, page_table, p_prompt, p_sample, w_in, b_cg, g_qn, w_uq, g_kvn, w_uk, w_uv, w_pa, w_dw, b_dw, g_cn, b_cn, w_pb, w_o, g_ln1, b_ln1, w_router, b_router, w_eg, w_eu, w_ed, w_sg, w_su, w_sd, g_ln2, b_ln2, w_ple, w_pleg):
    b, s, d = x_prompt.shape
    bd, t, _ = x_sample.shape
    depth = w_in.shape[0]
    n_p, n_s = b * s, bd * t
    n = n_p + n_s
    past = page_table.shape[1] * cache_ckv.shape[2]
    alpha = (2 * depth) ** 0.25
    hist = CONV_W - 1

    cos_p, sin_p = _rope_tables(jnp.arange(s, dtype=jnp.int32))
    cos_s, sin_s = _rope_tables(past + jnp.tile(jnp.arange(t, dtype=jnp.int32), bd))

    tm_p = _pick_tile(s, 256)
    tm_s = _pick_tile(n_s, 256)
    tm_n = _pick_tile(n, 512)
    tm_g = _pick_tile(n, 128)
    tq = _pick_tile(s, 512)
    pages_per_step = _pick_tile(page_table.shape[1], 16)
    n_blocks = -(-(n * TOP_K + N_EXPERTS * (EXPERT_ROWS - 1)) // EXPERT_ROWS)

    xp = x_prompt.reshape(n_p, d)
    xs = x_sample.reshape(n_s, d)
    outs = {k: [] for k in ("ckv_p", "kr_p", "conv_p", "ckv_s", "kr_s", "conv_s")}
    for i in range(depth):
        w = _layer_weights(i, w_in, b_cg, g_qn, w_uq, g_kvn, w_uk, w_uv, w_pa, w_dw, b_dw, g_cn, b_cn, w_pb,
                           w_o, g_ln1, b_ln1, w_router, b_router, w_sg, w_su, w_sd, g_ln2, b_ln2, w_ple, w_pleg)
        ckv_p, kr_p, q_p, k_p, v_p = _attn_proj(xp, cos_p, sin_p, w, tm_p)
        attn_p = _prefill_attention(q_p, k_p, v_p, b, s, tq)
        ckv_s, kr_s, q_s, _, _ = _attn_proj(xs, cos_s, sin_s, w, tm_s)
        q_lat = _head_matmul(q_s, w["wuk_t"], 2, BF16)
        q_lat = q_lat.reshape(bd, t, N_HEADS, KV_LORA).transpose(0, 2, 1, 3).reshape(bd, N_HEADS * t, KV_LORA)
        q_r = q_s.reshape(bd, t, N_HEADS, QK_PAD)[..., QK_NOPE:QK_NOPE + QK_ROPE]
        q_r = q_r.transpose(0, 2, 1, 3).reshape(bd, N_HEADS * t, QK_ROPE)
        o_lat = _decode_attention(q_lat, q_r, ckv_s.reshape(bd, t, KV_LORA), kr_s.reshape(bd, t, QK_ROPE),
                                  cache_ckv[i], jnp.swapaxes(cache_krope[i], 1, 2), page_table, pages_per_step)
        o_lat = o_lat.reshape(bd, N_HEADS, t, KV_LORA).transpose(0, 2, 1, 3).reshape(n_s, N_HEADS * KV_LORA)
        attn_s = _head_matmul(o_lat, w["wuv_h"], 1, BF16)
        ga_p, hb_p, ulast = _conv_prompt(xp.reshape(b, s, d), w, tm_p)
        state_t = state_conv[i].transpose(1, 0, 2)
        ga_s, hb_s, u_s = _conv_sample(xs.reshape(bd, t, d).transpose(1, 0, 2), state_t, w, _pick_tile(bd, 128))
        from_t = lambda a: a.transpose(1, 0, 2).reshape(n_s, d)
        x_all = jnp.concatenate([xp, xs], axis=0)
        x1, base, eidx, ew = _merge(jnp.concatenate([attn_p, attn_s], axis=0),
                                    jnp.concatenate([ga_p.reshape(n_p, d), from_t(ga_s)], axis=0),
                                    jnp.concatenate([hb_p.reshape(n_p, d), from_t(hb_s)], axis=0),
                                    x_all, w, alpha, tm_n)
        rank, counts = _ranks(eidx, tm_n)
        pcounts = (counts[0] + EXPERT_ROWS - 1) // EXPERT_ROWS * EXPERT_ROWS
        pends = jnp.cumsum(pcounts)
        pstart = (pends - pcounts).astype(jnp.int32)
        nused = (pends[-1] // EXPERT_ROWS).astype(jnp.int32)
        blk = jnp.minimum(jnp.arange(n_blocks, dtype=jnp.int32), nused - 1) * EXPERT_ROWS
        blk_e = jnp.minimum(jnp.sum(pends[None, :] <= blk[:, None], axis=1), N_EXPERTS - 1).astype(jnp.int32)
        dest = _destinations(eidx, rank, pstart.astype(F32).reshape(1, N_EXPERTS), tm_n)
        xsorted = _dispatch(dest, x1, pstart + counts[0], pcounts - counts[0], n_blocks * EXPERT_ROWS, tm_g)
        y = _expert_ffn(blk_e, nused.reshape(1), xsorted, w_eg[i], w_eu[i], w_ed[i])
        p_all = jnp.concatenate([p_prompt[i].reshape(n_p, PLE_DIM), p_sample[i].reshape(n_s, PLE_DIM)], axis=0)
        x_all = _combine(dest, y, ew, base, p_all, w, tm_g)
        xp, xs = x_all[:n_p], x_all[n_p:]
        outs["ckv_p"].append(ckv_p.reshape(b, s, KV_LORA))
        outs["kr_p"].append(kr_p.reshape(b, s, QK_ROPE))
        outs["conv_p"].append(ulast[:, HIST - hist:, :])
        outs["ckv_s"].append(ckv_s.reshape(bd, t, KV_LORA))
        outs["kr_s"].append(kr_s.reshape(bd, t, QK_ROPE))
        outs["conv_s"].append(jnp.concatenate([state_conv[i], u_s.transpose(1, 0, 2)], axis=1)[:, -hist:])
    return (xp.reshape(b, s, d), xs.reshape(bd, t, d), jnp.stack(outs["ckv_p"]), jnp.stack(outs["kr_p"]),
            jnp.stack(outs["conv_p"]), jnp.stack(outs["ckv_s"]), jnp.stack(outs["kr_s"]),
            jnp.stack(outs["conv_s"]))
```

```python
import functools
import math

import jax
import jax.numpy as jnp
from jax import lax
from jax.experimental import pallas as pl
from jax.experimental.pallas import tpu as pltpu

F32 = jnp.float32
BF16 = jnp.bfloat16

D_MODEL = 1024
N_HEADS = 8
QK_NOPE = 128
QK_ROPE = 64
V_HEAD = 128
Q_LORA = 384
KV_LORA = 256
ROPE_THETA = 10000.0
SM_SCALE = (QK_NOPE + QK_ROPE) ** -0.5
CONV_W = 31
N_EXPERTS = 256
TOP_K = 8
N_GROUPS = 8
TOPK_GROUPS = 4
GROUP_SIZE = N_EXPERTS // N_GROUPS
D_EXPERT = D_MODEL // 4
D_SHARED = D_MODEL // 4
ROUTED_SCALE = 2.5
PLE_DIM = 256
LN_EPS = 1e-5
RMS_EPS = 1e-6

LANES = 128
SUBLANES = 8
QK_PAD = 2 * LANES
V_PAD = 2 * LANES
HIST = 32
EXP2_SCALE = SM_SCALE * 1.4426950408889634
ATTN_ROW_CHUNK = 128
EXPERT_ROWS = 256
VMEM_LIMIT = 56 * 1024 * 1024

_NEG_INF = float("-inf")


def _cparams(*sem):
    return pltpu.CompilerParams(dimension_semantics=sem, vmem_limit_bytes=VMEM_LIMIT)


def _const_spec(shape):
    nd = len(shape)
    return pl.BlockSpec(shape, lambda *_: (0,) * nd)


def _dot(a, b):
    return jnp.dot(a, b, preferred_element_type=F32)


def _dot_t(a, b):
    return lax.dot_general(a, b, (((1,), (1,)), ((), ())), preferred_element_type=F32)


def _layernorm(x, g, b):
    mu = jnp.mean(x, axis=-1, keepdims=True)
    xc = x - mu
    var = jnp.mean(xc * xc, axis=-1, keepdims=True)
    return xc * lax.rsqrt(var + LN_EPS) * g + b


def _rmsnorm(x, g):
    return x * lax.rsqrt(jnp.mean(x * x, axis=-1, keepdims=True) + RMS_EPS) * g


def _sigmoid(x):
    return 1.0 / (1.0 + jnp.exp(-x))


def _silu(x):
    return x * _sigmoid(x)


def _attn_proj_kernel(x_ref, cos_ref, sin_ref, wq_ref, wkv_ref, wkr_ref, gq_ref, gkv_ref,
                      wuq_ref, wuqs_ref, wuk_ref, wuv_ref,
                      ckv_ref, kr_ref, q_ref, k_ref, v_ref):
    xb = x_ref[...].astype(BF16)
    cos = cos_ref[...]
    sin = sin_ref[...]
    cq = _rmsnorm(_dot(xb, wq_ref[...]), gq_ref[...]).astype(BF16)
    ckv = _rmsnorm(_dot(xb, wkv_ref[...]), gkv_ref[...])
    ckv_ref[...] = ckv
    ckv_b = ckv.astype(BF16)
    zkr = _dot(xb, wkr_ref[...])
    kr = zkr[:, :LANES] * cos + zkr[:, LANES:] * sin
    kr_ref[...] = kr[:, :QK_ROPE]
    kr_b = kr.astype(BF16)
    qa = _dot(cq, wuq_ref[...])
    qs = _dot(cq, wuqs_ref[...])
    kn = _dot(ckv_b, wuk_ref[...])
    vv = _dot(ckv_b, wuv_ref[...]).astype(BF16)
    ones = jnp.ones((vv.shape[0], V_PAD - V_HEAD), BF16)
    for h in range(N_HEADS):
        v_ref[:, h * V_PAD:h * V_PAD + V_HEAD] = vv[:, h * V_HEAD:(h + 1) * V_HEAD]
        v_ref[:, h * V_PAD + V_HEAD:(h + 1) * V_PAD] = ones
        c0 = h * QK_PAD
        q_ref[:, c0:c0 + LANES] = qa[:, c0:c0 + LANES].astype(BF16)
        q_ref[:, c0 + LANES:c0 + QK_PAD] = (
            qa[:, c0 + LANES:c0 + QK_PAD] * cos + qs[:, h * LANES:(h + 1) * LANES] * sin).astype(BF16)
        k_ref[:, c0:c0 + LANES] = kn[:, h * LANES:(h + 1) * LANES].astype(BF16)
        k_ref[:, c0 + LANES:c0 + QK_PAD] = kr_b


def _attn_proj(x, cos, sin, w, tm):
    n = x.shape[0]
    n_pos = cos.shape[0] // tm
    row = lambda i: (i, 0)
    pos_row = lambda i: (i % n_pos, 0)
    weights = (w["wq"], w["wkv"], w["wkr"], w["g_qn"], w["g_kvn"], w["wuq"], w["wuqs"], w["wuk"], w["wuv"])
    return pl.pallas_call(
        _attn_proj_kernel,
        grid=(n // tm,),
        in_specs=[pl.BlockSpec((tm, D_MODEL), row),
                  pl.BlockSpec((tm, LANES), pos_row),
                  pl.BlockSpec((tm, LANES), pos_row)] + [_const_spec(a.shape) for a in weights],
        out_specs=[pl.BlockSpec((tm, KV_LORA), row),
                   pl.BlockSpec((tm, QK_ROPE), row),
                   pl.BlockSpec((tm, N_HEADS * QK_PAD), row),
                   pl.BlockSpec((tm, N_HEADS * QK_PAD), row),
                   pl.BlockSpec((tm, N_HEADS * V_PAD), row)],
        out_shape=[jax.ShapeDtypeStruct((n, KV_LORA), F32),
                   jax.ShapeDtypeStruct((n, QK_ROPE), F32),
                   jax.ShapeDtypeStruct((n, N_HEADS * QK_PAD), BF16),
                   jax.ShapeDtypeStruct((n, N_HEADS * QK_PAD), BF16),
                   jax.ShapeDtypeStruct((n, N_HEADS * V_PAD), BF16)],
        compiler_params=_cparams("arbitrary"),
        name="attn_proj",
    )(x, cos, sin, *weights)


def _prefill_kernel(q_ref, k_ref, v_ref, o_ref, m_ref, acc_ref, s0_ref, s1_ref, p_ref, *, tq):
    qi = pl.program_id(2)
    q = q_ref[...]
    chunk = min(ATTN_ROW_CHUNK, tq)
    m_ref[...] = jnp.full(m_ref.shape, _NEG_INF, F32)
    acc_ref[...] = jnp.zeros(acc_ref.shape, F32)

    def scores(j, s_ref):
        start = pl.multiple_of(j * tq, tq)
        s_ref[...] = _dot_t(q, k_ref[pl.ds(start, tq), :])

    def consume(j, s_ref, masked):
        start = pl.multiple_of(j * tq, tq)
        for r0 in range(0, tq, chunk):
            rows = slice(r0, r0 + chunk)
            s = s_ref[rows, :] * EXP2_SCALE
            if masked:
                r = lax.broadcasted_iota(jnp.int32, (chunk, tq), 0) + r0
                c = lax.broadcasted_iota(jnp.int32, (chunk, tq), 1)
                s = jnp.where(c <= r, s, _NEG_INF)
            m_old = m_ref[rows, :]
            m_new = jnp.maximum(m_old, jnp.max(s, axis=-1, keepdims=True))
            p_ref[rows, :] = jnp.exp2(s - m_new).astype(BF16)
            acc_ref[rows, :] = acc_ref[rows, :] * jnp.exp2(m_old - m_new)
            m_ref[rows, :] = m_new
        acc_ref[...] += _dot(p_ref[...], v_ref[pl.ds(start, tq), :])

    scores(0, s0_ref)

    def body(i, carry):
        j = 2 * i
        scores(j + 1, s1_ref)
        consume(j, s0_ref, False)
        scores(j + 2, s0_ref)
        consume(j + 1, s1_ref, False)
        return carry

    lax.fori_loop(0, qi // 2, body, 0)

    @pl.when(qi % 2 == 0)
    def _():
        consume(qi, s0_ref, True)

    @pl.when(qi % 2 == 1)
    def _():
        scores(qi, s1_ref)
        consume(qi - 1, s0_ref, False)
        consume(qi, s1_ref, True)

    acc = acc_ref[...]
    o_ref[...] = (acc[:, :V_HEAD] / acc[:, V_HEAD:]).astype(o_ref.dtype)


def _prefill_attention(q, k, v, b, s, tq):
    q3 = q.reshape(b, s, N_HEADS * QK_PAD)
    k3 = k.reshape(b, s, N_HEADS * QK_PAD)
    v3 = v.reshape(b, s, N_HEADS * V_PAD)
    out = pl.pallas_call(
        functools.partial(_prefill_kernel, tq=tq),
        grid=(b, N_HEADS, s // tq),
        in_specs=[pl.BlockSpec((None, tq, QK_PAD), lambda bi, h, i: (bi, i, h)),
                  pl.BlockSpec((None, s, QK_PAD), lambda bi, h, i: (bi, 0, h)),
                  pl.BlockSpec((None, s, V_PAD), lambda bi, h, i: (bi, 0, h))],
        out_specs=pl.BlockSpec((None, tq, V_HEAD), lambda bi, h, i: (bi, i, h)),
        out_shape=jax.ShapeDtypeStruct((b, s, N_HEADS * V_HEAD), BF16),
        scratch_shapes=[pltpu.VMEM((tq, 1), F32), pltpu.VMEM((tq, V_PAD), F32),
                        pltpu.VMEM((tq, tq), F32), pltpu.VMEM((tq, tq), F32), pltpu.VMEM((tq, tq), BF16)],
        compiler_params=_cparams("arbitrary", "arbitrary", "arbitrary"),
        name="prefill_attention",
    )(q3, k3, v3)
    return out.reshape(b * s, N_HEADS * V_HEAD)


def _head_matmul_kernel(a_ref, w_ref, o_ref):
    o_ref[...] = _dot(a_ref[...].astype(BF16), w_ref[...]).astype(o_ref.dtype)


def _head_matmul(a, w, col_stride, out_dtype):
    n = a.shape[0]
    _, kk, mm = w.shape
    return pl.pallas_call(
        _head_matmul_kernel,
        grid=(N_HEADS,),
        in_specs=[pl.BlockSpec((n, kk), lambda h: (0, h * col_stride)),
                  pl.BlockSpec((None, kk, mm), lambda h: (h, 0, 0))],
        out_specs=pl.BlockSpec((n, mm), lambda h: (0, h)),
        out_shape=jax.ShapeDtypeStruct((n, N_HEADS * mm), out_dtype),
        compiler_params=_cparams("arbitrary"),
        name="head_matmul",
    )(a, w)


def _decode_kernel(pt_ref, ql_ref, qr_ref, cn_ref, kn_ref, *rest, pages_per_step, t):
    del pt_ref
    pg = pages_per_step
    ck_refs = rest[:pg]
    kr_refs = rest[pg:2 * pg]
    o_ref, m_ref, l_ref, acc_ref = rest[2 * pg:]
    j = pl.program_id(1)
    ql = ql_ref[...]
    qr = qr_ref[...]
    rows = ql.shape[0]

    @pl.when(j == 0)
    def _():
        cn = cn_ref[...]
        cn_b = cn.astype(BF16)
        s = (_dot_t(ql, cn_b) + _dot_t(qr, kn_ref[...].astype(BF16))) * SM_SCALE
        ti = lax.broadcasted_iota(jnp.int32, (rows, t), 0) % t
        ui = lax.broadcasted_iota(jnp.int32, (rows, t), 1)
        s = jnp.where(ui <= ti, s, _NEG_INF)
        m = jnp.max(s, axis=-1, keepdims=True)
        p = jnp.exp(s - m)
        m_ref[...] = m
        l_ref[...] = jnp.sum(p, axis=-1, keepdims=True)
        acc_ref[...] = _dot(p.astype(BF16), cn_b)

    cks = [r[...].astype(BF16) for r in ck_refs]
    s = jnp.concatenate(
        [_dot_t(ql, ck) + _dot(qr, kr[...].astype(BF16)) for ck, kr in zip(cks, kr_refs)],
        axis=-1) * SM_SCALE
    m_old = m_ref[...]
    m_new = jnp.maximum(m_old, jnp.max(s, axis=-1, keepdims=True))
    corr = jnp.exp(m_old - m_new)
    p = jnp.exp(s - m_new)
    l_ref[...] = l_ref[...] * corr + jnp.sum(p, axis=-1, keepdims=True)
    pb = p.astype(BF16)
    page = cks[0].shape[0]
    pv = _dot(pb[:, :page], cks[0])
    for r in range(1, pg):
        pv = pv + _dot(pb[:, r * page:(r + 1) * page], cks[r])
    acc_ref[...] = acc_ref[...] * corr + pv
    m_ref[...] = m_new

    @pl.when(j == pl.num_programs(1) - 1)
    def _():
        o_ref[...] = acc_ref[...] / l_ref[...]


def _decode_attention(q_lat, q_rope, ckv_new, kr_new, cache_ckv, cache_kr, page_table, pages_per_step):
    bd, rows, _ = q_lat.shape
    t = ckv_new.shape[1]
    n_pages = page_table.shape[1]
    page = cache_ckv.shape[1]
    pg = pages_per_step
    seq = lambda b, j, pt: (b, 0, 0)

    def page_map(r):
        return lambda b, j, pt: (pt[b, j * pg + r], 0, 0)

    grid_spec = pltpu.PrefetchScalarGridSpec(
        num_scalar_prefetch=1,
        grid=(bd, n_pages // pg),
        in_specs=[pl.BlockSpec((None, rows, KV_LORA), seq),
                  pl.BlockSpec((None, rows, QK_ROPE), seq),
                  pl.BlockSpec((None, t, KV_LORA), seq),
                  pl.BlockSpec((None, t, QK_ROPE), seq)]
                 + [pl.BlockSpec((None, page, KV_LORA), page_map(r)) for r in range(pg)]
                 + [pl.BlockSpec((None, QK_ROPE, page), page_map(r)) for r in range(pg)],
        out_specs=pl.BlockSpec((None, rows, KV_LORA), seq),
        scratch_shapes=[pltpu.VMEM((rows, 1), F32), pltpu.VMEM((rows, 1), F32),
                        pltpu.VMEM((rows, KV_LORA), F32)],
    )
    return pl.pallas_call(
        functools.partial(_decode_kernel, pages_per_step=pg, t=t),
        grid_spec=grid_spec,
        out_shape=jax.ShapeDtypeStruct((bd, rows, KV_LORA), F32),
        compiler_params=_cparams("arbitrary", "arbitrary"),
        name="decode_attention",
    )(page_table, q_lat, q_rope, ckv_new, kr_new, *([cache_ckv] * pg), *([cache_kr] * pg))


def _conv_tail(cv, zga, zgb, gcn_ref, bcn_ref, wpb_ref, ga_ref, hb_ref):
    act = _silu(_layernorm(cv, gcn_ref[...], bcn_ref[...])).astype(BF16)
    hb_ref[...] = _sigmoid(zgb) * _dot(act, wpb_ref[...])
    ga_ref[...] = _sigmoid(zga)


def _conv_prompt_kernel(x_ref, wcg_ref, bcg_ref, wdw_ref, bdw_ref, gcn_ref, bcn_ref, wpb_ref,
                        ga_ref, hb_ref, ulast_ref, ush_ref, cv_ref, *, tm, chunk):
    i = pl.program_id(1)
    rows = HIST + tm

    @pl.when(i == 0)
    def _():
        ush_ref[0, 0:HIST, :] = jnp.zeros((HIST, D_MODEL), F32)

    z = _dot(x_ref[...].astype(BF16), wcg_ref[...]) + bcg_ref[...]
    u = z[:, :D_MODEL] * _sigmoid(z[:, D_MODEL:2 * D_MODEL])
    ush_ref[0, HIST:rows, :] = u
    ulast_ref[...] = u[tm - HIST:tm, :]
    for p in range(1, SUBLANES):
        ush_ref[p, 0:rows - SUBLANES, :] = ush_ref[0, p:p + rows - SUBLANES, :]
    off = HIST - (CONV_W - 1)
    for c in range(tm // chunk):
        r0 = c * chunk
        acc = jnp.broadcast_to(bdw_ref[...], (chunk, D_MODEL))
        for jt in range(CONV_W):
            a, p = divmod(off + jt, SUBLANES)
            start = r0 + a * SUBLANES
            acc = acc + wdw_ref[jt:jt + 1, :] * ush_ref[p, start:start + chunk, :]
        cv_ref[r0:r0 + chunk, :] = acc
    ush_ref[0, 0:HIST, :] = ush_ref[0, tm:rows, :]
    _conv_tail(cv_ref[...], z[:, 2 * D_MODEL:3 * D_MODEL], z[:, 3 * D_MODEL:], gcn_ref, bcn_ref, wpb_ref,
               ga_ref, hb_ref)


def _conv_prompt(x3, w, tm):
    b, s, _ = x3.shape
    weights = (w["wcg"], w["b_cg"], w["w_dw"], w["b_dw"], w["g_cn"], w["b_cn"], w["wpb"])
    row = lambda bi, i: (bi, i, 0)
    return pl.pallas_call(
        functools.partial(_conv_prompt_kernel, tm=tm, chunk=32),
        grid=(b, s // tm),
        in_specs=[pl.BlockSpec((None, tm, D_MODEL), row)] + [_const_spec(a.shape) for a in weights],
        out_specs=[pl.BlockSpec((None, tm, D_MODEL), row),
                   pl.BlockSpec((None, tm, D_MODEL), row),
                   pl.BlockSpec((None, HIST, D_MODEL), lambda bi, i: (bi, 0, 0))],
        out_shape=[jax.ShapeDtypeStruct((b, s, D_MODEL), F32),
                   jax.ShapeDtypeStruct((b, s, D_MODEL), F32),
                   jax.ShapeDtypeStruct((b, HIST, D_MODEL), F32)],
        scratch_shapes=[pltpu.VMEM((SUBLANES, HIST + tm, D_MODEL), F32), pltpu.VMEM((tm, D_MODEL), F32)],
        compiler_params=_cparams("arbitrary", "arbitrary"),
        name="conv_prompt",
    )(x3, *weights)


def _conv_sample_kernel(x_ref, st_ref, wcg_ref, bcg_ref, wdw_ref, bdw_ref, gcn_ref, bcn_ref, wpb_ref,
                        ga_ref, hb_ref, u_ref, *, t):
    zs = []
    for ti in range(t):
        z = _dot(x_ref[ti].astype(BF16), wcg_ref[...]) + bcg_ref[...]
        zs.append(z)
        u_ref[ti] = z[:, :D_MODEL] * _sigmoid(z[:, D_MODEL:2 * D_MODEL])
    hist = CONV_W - 1
    for ti in range(t):
        acc = jnp.broadcast_to(bdw_ref[...], u_ref.shape[1:])
        for jt in range(CONV_W):
            p = ti + jt
            src = st_ref[p] if p < hist else u_ref[p - hist]
            acc = acc + wdw_ref[jt:jt + 1, :] * src
        z = zs[ti]
        _conv_tail(acc, z[:, 2 * D_MODEL:3 * D_MODEL], z[:, 3 * D_MODEL:], gcn_ref, bcn_ref, wpb_ref,
                   ga_ref.at[ti], hb_ref.at[ti])


def _conv_sample(x_t, state_t, w, tb):
    t, bd, _ = x_t.shape
    hist = state_t.shape[0]
    weights = (w["wcg"], w["b_cg"], w["w_dw"], w["b_dw"], w["g_cn"], w["b_cn"], w["wpb"])
    blk = lambda i: (0, i, 0)
    return pl.pallas_call(
        functools.partial(_conv_sample_kernel, t=t),
        grid=(bd // tb,),
        in_specs=[pl.BlockSpec((t, tb, D_MODEL), blk), pl.BlockSpec((hist, tb, D_MODEL), blk)]
                 + [_const_spec(a.shape) for a in weights],
        out_specs=[pl.BlockSpec((t, tb, D_MODEL), blk)] * 3,
        out_shape=[jax.ShapeDtypeStruct((t, bd, D_MODEL), F32)] * 3,
        compiler_params=_cparams("arbitrary"),
        name="conv_sample",
    )(x_t, state_t, *weights)


def _first_argmax(vals, lane):
    m = jnp.max(vals, axis=-1, keepdims=True)
    idx = jnp.min(jnp.where(vals == m, lane, float(vals.shape[-1])), axis=-1, keepdims=True)
    return m, idx


def _route(scores, bias, eidx_ref, ew_ref):
    tm = scores.shape[0]
    lane = lax.broadcasted_iota(jnp.int32, (tm, N_EXPERTS), 1).astype(F32)
    group = jnp.floor(lane * (1.0 / GROUP_SIZE))
    sel = scores + bias
    gscore = []
    for g in range(N_GROUPS):
        vals = jnp.where(group == g, sel, _NEG_INF)
        m1, i1 = _first_argmax(vals, lane)
        m2 = jnp.max(jnp.where(lane == i1, _NEG_INF, vals), axis=-1, keepdims=True)
        gscore.append(m1 + m2)
    vals = jnp.full((tm, N_EXPERTS), _NEG_INF, F32)
    for g in range(N_GROUPS):
        beaten = jnp.zeros((tm, 1), F32)
        for o in range(N_GROUPS):
            if o != g:
                ahead = gscore[o] >= gscore[g] if o < g else gscore[o] > gscore[g]
                beaten = beaten + jnp.where(ahead, 1.0, 0.0)
        kept = jnp.where(beaten < TOPK_GROUPS, 1.0, 0.0)
        vals = jnp.where((group == g) & (kept > 0.5), sel, vals)
    out_lane = lax.broadcasted_iota(jnp.int32, (tm, TOP_K), 1)
    eidx = jnp.zeros((tm, TOP_K), F32)
    ew = jnp.zeros((tm, TOP_K), F32)
    for k in range(TOP_K):
        _, ik = _first_argmax(vals, lane)
        hit = lane == ik
        wk = jnp.sum(jnp.where(hit, scores, 0.0), axis=-1, keepdims=True)
        vals = jnp.where(hit, _NEG_INF, vals)
        eidx = jnp.where(out_lane == k, ik, eidx)
        ew = jnp.where(out_lane == k, wk, ew)
    eidx_ref[...] = eidx.astype(jnp.int32)
    ew_ref[...] = ew / jnp.sum(ew, axis=-1, keepdims=True) * ROUTED_SCALE


_HI_MASK = 0xFFFF0000


def _pack_bf16_pair(x):
    half = x.shape[1] // 2
    lo = lax.bitcast_convert_type(x[:, :half].astype(BF16).astype(F32), jnp.uint32) >> 16
    hi = lax.bitcast_convert_type(x[:, half:].astype(BF16).astype(F32), jnp.uint32) & jnp.uint32(_HI_MASK)
    return lo | hi


def _unpack_bf16_pair(u):
    lo = lax.bitcast_convert_type(u << 16, F32).astype(BF16)
    hi = lax.bitcast_convert_type(u & jnp.uint32(_HI_MASK), F32).astype(BF16)
    return lo, hi


def _prompt_or_sample(p_ref, s_ref, is_sample):
    return jnp.where(is_sample, s_ref[...], p_ref[...])


def _merge_kernel(attn_p, attn_s, ga_p, ga_s, hb_p, hb_s, x_p, x_s, wpa_ref, wo_ref, g1_ref, b1_ref, wsgu_ref,
                  wsd_ref, wr_ref, br_ref, x1p_ref, base_ref, eidx_ref, ew_ref, *, alpha, prompt_tiles):
    is_sample = pl.program_id(0) >= prompt_tiles
    attn = _prompt_or_sample(attn_p, attn_s, is_sample)
    h = _prompt_or_sample(ga_p, ga_s, is_sample) * _dot(attn, wpa_ref[...]) + _prompt_or_sample(hb_p, hb_s, is_sample)
    x = _prompt_or_sample(x_p, x_s, is_sample)
    x1 = _layernorm(alpha * x + _dot(h.astype(BF16), wo_ref[...]), g1_ref[...], b1_ref[...])
    x1p_ref[...] = _pack_bf16_pair(x1)
    x1b = x1.astype(BF16)
    gu = _dot(x1b, wsgu_ref[...])
    mid = (_silu(gu[:, :D_SHARED]) * gu[:, D_SHARED:]).astype(BF16)
    base_ref[...] = alpha * x1 + _dot(mid, wsd_ref[...])
    scores = _sigmoid(_dot(x1b, wr_ref[...]))
    _route(scores, br_ref[...], eidx_ref, ew_ref)


def _split_specs(block, prompt_tiles, shift=0):
    nd = len(block)
    rest = (0,) * (nd - 1)
    tile = lambda i: jnp.maximum(i - shift, 0)
    return [pl.BlockSpec(block, lambda i: (jnp.minimum(tile(i), prompt_tiles - 1),) + rest),
            pl.BlockSpec(block, lambda i: (jnp.maximum(tile(i) - prompt_tiles, 0),) + rest)]


def _merge(attn, ga, hb, x, w, alpha, tm):
    n_p, n_s = x[0].shape[0], x[1].shape[0]
    n = n_p + n_s
    prompt_tiles = n_p // tm
    row = lambda i: (i, 0)
    weights = (w["wpa"], w["wo"], w["g_ln1"], w["b_ln1"], w["wsgu"], w["wsd"], w["wr"], w["b_router"])
    small = pl.BlockSpec((tm, TOP_K), row)
    return pl.pallas_call(
        functools.partial(_merge_kernel, alpha=alpha, prompt_tiles=prompt_tiles),
        grid=(n // tm,),
        in_specs=_split_specs((tm, D_MODEL), prompt_tiles) * 4 + [_const_spec(a.shape) for a in weights],
        out_specs=[pl.BlockSpec((tm, D_MODEL // 2), row), pl.BlockSpec((tm, D_MODEL), row), small, small],
        out_shape=[jax.ShapeDtypeStruct((n, D_MODEL // 2), jnp.uint32), jax.ShapeDtypeStruct((n, D_MODEL), F32),
                   jax.ShapeDtypeStruct((n, TOP_K), jnp.int32), jax.ShapeDtypeStruct((n, TOP_K), F32)],
        compiler_params=_cparams("arbitrary"),
        name="merge_route",
    )(*attn, *ga, *hb, *x, *weights)


def _onehots(eidx, tm):
    lane = lax.broadcasted_iota(jnp.int32, (tm, N_EXPERTS), 1)
    return [lane == eidx[:, k:k + 1] for k in range(TOP_K)]


def _rank_kernel(eidx_ref, rank_ref, count_ref, carry_ref, *, tm):
    @pl.when(pl.program_id(0) == 0)
    def _():
        carry_ref[...] = jnp.zeros(carry_ref.shape, F32)

    hits = _onehots(eidx_ref[...], tm)
    member = hits[0]
    for hk in hits[1:]:
        member = member | hk
    member_b = jnp.where(member, 1.0, 0.0).astype(BF16)
    r = lax.broadcasted_iota(jnp.int32, (tm, tm), 0)
    c = lax.broadcasted_iota(jnp.int32, (tm, tm), 1)
    below = jnp.where(c < r, 1.0, 0.0).astype(BF16)
    before = _dot(below, member_b) + carry_ref[...]
    out_lane = lax.broadcasted_iota(jnp.int32, (tm, TOP_K), 1)
    rank = jnp.zeros((tm, TOP_K), F32)
    for k in range(TOP_K):
        rk = jnp.sum(jnp.where(hits[k], before, 0.0), axis=-1, keepdims=True)
        rank = jnp.where(out_lane == k, rk, rank)
    rank_ref[...] = rank.astype(jnp.int32)
    total = carry_ref[...] + jnp.sum(member_b.astype(F32), axis=0, keepdims=True)
    carry_ref[...] = total
    count_ref[...] = total.astype(jnp.int32)


def _ranks(eidx, tm):
    n = eidx.shape[0]
    return pl.pallas_call(
        functools.partial(_rank_kernel, tm=tm),
        grid=(n // tm,),
        in_specs=[pl.BlockSpec((tm, TOP_K), lambda i: (i, 0))],
        out_specs=[pl.BlockSpec((tm, TOP_K), lambda i: (i, 0)), _const_spec((1, N_EXPERTS))],
        out_shape=[jax.ShapeDtypeStruct((n, TOP_K), jnp.int32),
                   jax.ShapeDtypeStruct((1, N_EXPERTS), jnp.int32)],
        scratch_shapes=[pltpu.VMEM((1, N_EXPERTS), F32)],
        compiler_params=_cparams("arbitrary"),
        name="expert_ranks",
    )(eidx)


def _dest_kernel(eidx_ref, rank_ref, pstart_ref, dest_ref, *, tm):
    hits = _onehots(eidx_ref[...], tm)
    pstart = pstart_ref[...]
    out_lane = lax.broadcasted_iota(jnp.int32, (tm, TOP_K), 1)
    base = jnp.zeros((tm, TOP_K), F32)
    for k in range(TOP_K):
        bk = jnp.sum(jnp.where(hits[k], pstart, 0.0), axis=-1, keepdims=True)
        base = jnp.where(out_lane == k, bk, base)
    dest_ref[...] = base.astype(jnp.int32) + rank_ref[...]


def _destinations(eidx, rank, pstart, tm):
    n = eidx.shape[0]
    row = pl.BlockSpec((tm, TOP_K), lambda i: (i, 0))
    return pl.pallas_call(
        functools.partial(_dest_kernel, tm=tm),
        grid=(n // tm,),
        in_specs=[row, row, _const_spec((1, N_EXPERTS))],
        out_specs=row,
        out_shape=jax.ShapeDtypeStruct((n, TOP_K), jnp.int32),
        compiler_params=_cparams("arbitrary"),
        name="expert_dest",
    )(eidx, rank, pstart)


def _row_copy(src_ref, src_row, dst_ref, dst_row, sem):
    return pltpu.make_async_copy(src_ref.at[pl.ds(src_row, 1), :], dst_ref.at[pl.ds(dst_row, 1), :], sem)


_PAD_PIECES = tuple(EXPERT_ROWS >> (bit + 1) for bit in range(EXPERT_ROWS.bit_length() - 1)
                    if EXPERT_ROWS >> (bit + 1) >= SUBLANES)


def _zero_fill(pad_start_ref, pad_count_ref, zero_ref, xs_hbm, sem, wait):
    def go(cp):
        if wait:
            cp.wait()
        else:
            cp.start()

    def per_expert(e, carry):
        start = pad_start_ref[e]
        lead = (-start) & (SUBLANES - 1)
        for r in range(SUBLANES - 1):
            @pl.when(r < lead)
            def _(r=r):
                go(_row_copy(zero_ref, 0, xs_hbm, start + r, sem))

        rest = pad_count_ref[e] - lead
        pos = start + lead
        for size in _PAD_PIECES:
            has = (rest & size) != 0

            @pl.when(has)
            def _(pos=pos, size=size):
                dst = xs_hbm.at[pl.ds(pl.multiple_of(pos, SUBLANES), size), :]
                go(pltpu.make_async_copy(zero_ref.at[pl.ds(0, size), :], dst, sem))

            pos = pos + jnp.where(has, size, 0)
        return carry

    lax.fori_loop(0, N_EXPERTS, per_expert, 0)

    piece = zero_ref.shape[0]

    def tail(j, carry):
        dst = xs_hbm.at[pl.ds(pl.multiple_of(j * piece, piece), piece), :]
        go(pltpu.make_async_copy(zero_ref, dst, sem))
        return carry

    end = pad_start_ref[N_EXPERTS - 1] + pad_count_ref[N_EXPERTS - 1]
    lax.fori_loop(end // piece, xs_hbm.shape[0] // piece, tail, 0)


def _wait_row_copies(src_ref, dst_ref, sem, count):
    def body(r, carry):
        for _ in range(TOP_K):
            _row_copy(src_ref, 0, dst_ref, 0, sem).wait()
        return carry

    lax.fori_loop(0, count, body, 0)


def _dispatch_kernel(pad_start_ref, pad_count_ref, dest_hbm, x_ref, xs_hbm, dest_smem, zero_ref, isem, sem,
                     zsem, *, tm):
    i = pl.program_id(0)
    last = pl.num_programs(0) - 1
    per_tile = tm * TOP_K
    slot = i % 2

    def idx_copy(step, s):
        dst = dest_smem.at[pl.ds(pl.multiple_of(s * per_tile, per_tile), per_tile)]
        return pltpu.make_async_copy(dest_hbm.at[step], dst, isem.at[s])

    @pl.when(i == 0)
    def _():
        zero_ref[...] = jnp.zeros(zero_ref.shape, zero_ref.dtype)
        _zero_fill(pad_start_ref, pad_count_ref, zero_ref, xs_hbm, zsem, wait=False)
        idx_copy(0, 0).start()

    @pl.when(i < last)
    def _():
        idx_copy(i + 1, 1 - slot).start()

    idx_copy(i, slot).wait()
    first = slot * per_tile

    def issue(r, carry):
        for k in range(TOP_K):
            _row_copy(x_ref, r, xs_hbm, dest_smem[first + r * TOP_K + k], sem).start()
        return carry

    lax.fori_loop(0, tm, issue, 0)
    _wait_row_copies(x_ref, xs_hbm, sem, tm)

    @pl.when(i == last)
    def _():
        _zero_fill(pad_start_ref, pad_count_ref, zero_ref, xs_hbm, zsem, wait=True)


def _dispatch(dest, x1p, pad_start, pad_count, rows, tm):
    n, width = x1p.shape
    dest2 = dest.reshape(n // tm, tm * TOP_K)
    grid_spec = pltpu.PrefetchScalarGridSpec(
        num_scalar_prefetch=2,
        grid=(n // tm,),
        in_specs=[pl.BlockSpec(memory_space=pl.ANY),
                  pl.BlockSpec((tm, width), lambda i, ps, pc: (i, 0))],
        out_specs=pl.BlockSpec(memory_space=pl.ANY),
        scratch_shapes=[pltpu.SMEM((2 * tm * TOP_K,), jnp.int32), pltpu.VMEM((_PAD_PIECES[0], width), x1p.dtype),
                        pltpu.SemaphoreType.DMA((2,)), pltpu.SemaphoreType.DMA, pltpu.SemaphoreType.DMA],
    )
    return pl.pallas_call(
        functools.partial(_dispatch_kernel, tm=tm),
        grid_spec=grid_spec,
        out_shape=jax.ShapeDtypeStruct((rows, width), x1p.dtype),
        compiler_params=_cparams("arbitrary"),
        name="dispatch",
    )(pad_start, pad_count, dest2, x1p)


def _expert_kernel(blk_e_ref, nused_ref, xs_ref, wg_ref, wu_ref, wd_ref, y_ref):
    del blk_e_ref
    g = pl.program_id(0)

    @pl.when(g < nused_ref[0])
    def _():
        lo, hi = _unpack_bf16_pair(xs_ref[...])
        half = D_MODEL // 2

        def proj(w_ref):
            return _dot(lo, w_ref[:half, :].astype(BF16)) + _dot(hi, w_ref[half:, :].astype(BF16))

        mid = (_silu(proj(wg_ref)) * proj(wu_ref)).astype(BF16)
        y_ref[...] = _dot(mid, wd_ref[...].astype(BF16))

    @pl.when(g >= nused_ref[0])
    def _():
        y_ref[...] = jnp.zeros(y_ref.shape, F32)


def _expert_ffn(blk_e, nused, xs, w_eg, w_eu, w_ed):
    rows = xs.shape[0]
    n_blocks = rows // EXPERT_ROWS
    row = lambda g, be, nu: (g, 0)
    in_row = lambda g, be, nu: (jnp.minimum(g, nu[0] - 1), 0)
    wsel = lambda g, be, nu: (be[g], 0, 0)
    grid_spec = pltpu.PrefetchScalarGridSpec(
        num_scalar_prefetch=2,
        grid=(n_blocks,),
        in_specs=[pl.BlockSpec((EXPERT_ROWS, xs.shape[1]), in_row),
                  pl.BlockSpec((None, D_MODEL, D_EXPERT), wsel),
                  pl.BlockSpec((None, D_MODEL, D_EXPERT), wsel),
                  pl.BlockSpec((None, D_EXPERT, D_MODEL), wsel)],
        out_specs=pl.BlockSpec((EXPERT_ROWS, D_MODEL), row),
    )
    return pl.pallas_call(
        _expert_kernel,
        grid_spec=grid_spec,
        out_shape=jax.ShapeDtypeStruct((rows, D_MODEL), F32),
        compiler_params=_cparams("arbitrary"),
        name="expert_ffn",
    )(blk_e, nused, xs, w_eg, w_eu, w_ed)


def _combine_kernel(dest_hbm, y_hbm, ew_ref, base_ref, pp_ref, ps_ref, g2_ref, b2_ref, wple_ref, wpleg_ref,
                    out_ref, dest_smem, ybuf_ref, isem, sem, *, tm, prompt_tiles):
    j = pl.program_id(0)
    tiles = pl.num_programs(0) - 1
    per_tile = tm * TOP_K

    def idx_copy(tile):
        s = tile % 2
        dst = dest_smem.at[pl.ds(pl.multiple_of(s * per_tile, per_tile), per_tile)]
        return pltpu.make_async_copy(dest_hbm.at[tile], dst, isem.at[s])

    @pl.when(j == 0)
    def _():
        idx_copy(0).start()

    @pl.when(j + 1 < tiles)
    def _():
        idx_copy(j + 1).start()

    @pl.when(j < tiles)
    def _():
        idx_copy(j).wait()
        s = j % 2
        first = s * per_tile

        def issue(r, carry):
            for k in range(TOP_K):
                _row_copy(y_hbm, dest_smem[first + r * TOP_K + k], ybuf_ref.at[s, k], r, sem.at[s]).start()
            return carry

        lax.fori_loop(0, tm, issue, 0)

    @pl.when(j > 0)
    def _():
        c = j - 1
        s = c % 2
        _wait_row_copies(y_hbm, ybuf_ref.at[s, 0], sem.at[s], tm)
        ew = ew_ref[...]
        routed = ew[:, 0:1] * ybuf_ref[s, 0]
        for k in range(1, TOP_K):
            routed = routed + ew[:, k:k + 1] * ybuf_ref[s, k]
        x2 = _layernorm(base_ref[...] + routed, g2_ref[...], b2_ref[...])
        p = _prompt_or_sample(pp_ref, ps_ref, c >= prompt_tiles)
        emb = _dot(p.astype(BF16), wple_ref[...])
        out_ref[...] = x2 + emb * _sigmoid(_dot(x2.astype(BF16), wpleg_ref[...]))


def _combine(dest, y, ew, base, p, w, tm):
    n = base.shape[0]
    prompt_tiles = p[0].shape[0] // tm
    dest2 = dest.reshape(n // tm, tm * TOP_K)
    row = lambda j: (jnp.maximum(j - 1, 0), 0)
    weights = (w["g_ln2"], w["b_ln2"], w["wple"], w["wpleg"])
    return pl.pallas_call(
        functools.partial(_combine_kernel, tm=tm, prompt_tiles=prompt_tiles),
        grid=(n // tm + 1,),
        in_specs=[pl.BlockSpec(memory_space=pl.ANY), pl.BlockSpec(memory_space=pl.ANY),
                  pl.BlockSpec((tm, TOP_K), row), pl.BlockSpec((tm, D_MODEL), row)]
                 + _split_specs((tm, PLE_DIM), prompt_tiles, shift=1) + [_const_spec(a.shape) for a in weights],
        out_specs=pl.BlockSpec((tm, D_MODEL), row),
        out_shape=jax.ShapeDtypeStruct((n, D_MODEL), F32),
        scratch_shapes=[pltpu.SMEM((2 * tm * TOP_K,), jnp.int32), pltpu.VMEM((2, TOP_K, tm, D_MODEL), F32),
                        pltpu.SemaphoreType.DMA((2,)), pltpu.SemaphoreType.DMA((2,))],
        compiler_params=_cparams("arbitrary"),
        name="combine",
    )(dest2, y, ew, base, *p, *weights)


def _rope_tables(pos):
    half = QK_ROPE // 2
    inv = ROPE_THETA ** (-jnp.arange(half, dtype=F32) / half)
    ang = pos.astype(F32)[:, None] * inv[None, :]
    reps = LANES // half
    return jnp.tile(jnp.cos(ang), (1, reps)), jnp.tile(jnp.sin(ang), (1, reps))


def _swap_halves(w):
    half = w.shape[-1] // 2
    return jnp.concatenate([-w[..., half:], w[..., :half]], axis=-1)


def _layer_weights(i, w_in, b_cg, g_qn, w_uq, g_kvn, w_uk, w_uv, w_pa, w_dw, b_dw, g_cn, b_cn, w_pb, w_o,
                   g_ln1, b_ln1, w_router, b_router, w_sg, w_su, w_sd, g_ln2, b_ln2, w_ple, w_pleg):
    o1, o2, o3 = Q_LORA, Q_LORA + KV_LORA, Q_LORA + KV_LORA + QK_ROPE
    win = w_in[i]
    row = lambda v: v[i].reshape(1, -1)
    wkr = win[:, o2:o3]
    zpad = jnp.zeros((D_MODEL, LANES - QK_ROPE), F32)
    uq = w_uq[i].reshape(Q_LORA, N_HEADS, QK_NOPE + QK_ROPE)
    uq_n, uq_r = uq[..., :QK_NOPE], uq[..., QK_NOPE:]
    hpad = jnp.zeros((Q_LORA, N_HEADS, LANES - QK_ROPE), F32)
    return {
        "wq": win[:, :o1].astype(BF16),
        "wkv": win[:, o1:o2].astype(BF16),
        "wkr": jnp.concatenate([wkr, zpad, _swap_halves(wkr), zpad], axis=1).astype(BF16),
        "wcg": win[:, o3:].astype(BF16),
        "b_cg": row(b_cg),
        "g_qn": row(g_qn),
        "g_kvn": row(g_kvn),
        "wuq": jnp.concatenate([uq_n, uq_r, hpad], axis=-1).reshape(Q_LORA, N_HEADS * QK_PAD).astype(BF16),
        "wuqs": jnp.concatenate([_swap_halves(uq_r), hpad], axis=-1).reshape(Q_LORA, N_HEADS * LANES).astype(BF16),
        "wuk": w_uk[i].reshape(KV_LORA, N_HEADS * QK_NOPE).astype(BF16),
        "wuv": w_uv[i].reshape(KV_LORA, N_HEADS * V_HEAD).astype(BF16),
        "wuk_t": w_uk[i].transpose(1, 2, 0).astype(BF16),
        "wuv_h": w_uv[i].transpose(1, 0, 2).astype(BF16),
        "wpa": w_pa[i].astype(BF16),
        "w_dw": w_dw[i],
        "b_dw": row(b_dw),
        "g_cn": row(g_cn),
        "b_cn": row(b_cn),
        "wpb": w_pb[i].astype(BF16),
        "wo": w_o[i].astype(BF16),
        "g_ln1": row(g_ln1),
        "b_ln1": row(b_ln1),
        "wr": w_router[i].astype(BF16),
        "b_router": row(b_router),
        "wsgu": jnp.concatenate([w_sg[i], w_su[i]], axis=1).astype(BF16),
        "wsd": w_sd[i].astype(BF16),
        "g_ln2": row(g_ln2),
        "b_ln2": row(b_ln2),
        "wple": w_ple[i].astype(BF16),
        "wpleg": w_pleg[i].astype(BF16),
    }


def _pick_tile(n, target):
    tm = min(n, target)
    while n % tm:
        tm //= 2
    return tm


def kernel(x_prompt, x_sample, cache_ckv, cache_krope, state_conv, page_table, p_prompt, p_sample, w_in, b_cg, g_qn, w_uq, g_kvn, w_uk, w_uv, w_pa, w_dw, b_dw, g_cn, b_cn, w_pb, w_o, g_ln1, b_ln1, w_router, b_router, w_eg, w_eu, w_ed, w_sg, w_su, w_sd, g_ln2, b_ln2, w_ple, w_pleg):
    b, s, d = x_prompt.shape
    bd, t, _ = x_sample.shape
    depth = w_in.shape[0]
    n_p, n_s = b * s, bd * t
    n = n_p + n_s
    past = page_table.shape[1] * cache_ckv.shape[2]
    alpha = (2 * depth) ** 0.25
    hist = CONV_W - 1

    cos_p, sin_p = _rope_tables(jnp.arange(s, dtype=jnp.int32))
    cos_s, sin_s = _rope_tables(past + jnp.tile(jnp.arange(t, dtype=jnp.int32), bd))

    tm_p = _pick_tile(s, 256)
    tm_s = _pick_tile(n_s, 256)
    common = math.gcd(n_p, n_s)
    tm_n = _pick_tile(common, 512)
    tm_m = _pick_tile(common, 256)
    tm_g = _pick_tile(common, 128)
    tq = _pick_tile(s, 512)
    pages_per_step = _pick_tile(page_table.shape[1], 16)
    n_blocks = -(-(n * TOP_K + N_EXPERTS * (EXPERT_ROWS - 1)) // EXPERT_ROWS)

    xp = x_prompt.reshape(n_p, d)
    xs = x_sample.reshape(n_s, d)
    outs = {k: [] for k in ("ckv_p", "kr_p", "conv_p", "ckv_s", "kr_s", "conv_s")}
    for i in range(depth):
        w = _layer_weights(i, w_in, b_cg, g_qn, w_uq, g_kvn, w_uk, w_uv, w_pa, w_dw, b_dw, g_cn, b_cn, w_pb,
                           w_o, g_ln1, b_ln1, w_router, b_router, w_sg, w_su, w_sd, g_ln2, b_ln2, w_ple, w_pleg)
        ckv_p, kr_p, q_p, k_p, v_p = _attn_proj(xp, cos_p, sin_p, w, tm_p)
        attn_p = _prefill_attention(q_p, k_p, v_p, b, s, tq)
        ckv_s, kr_s, q_s, _, _ = _attn_proj(xs, cos_s, sin_s, w, tm_s)
        q_lat = _head_matmul(q_s, w["wuk_t"], 2, BF16)
        q_lat = q_lat.reshape(bd, t, N_HEADS, KV_LORA).transpose(0, 2, 1, 3).reshape(bd, N_HEADS * t, KV_LORA)
        q_r = q_s.reshape(bd, t, N_HEADS, QK_PAD)[..., QK_NOPE:QK_NOPE + QK_ROPE]
        q_r = q_r.transpose(0, 2, 1, 3).reshape(bd, N_HEADS * t, QK_ROPE)
        o_lat = _decode_attention(q_lat, q_r, ckv_s.reshape(bd, t, KV_LORA), kr_s.reshape(bd, t, QK_ROPE),
                                  cache_ckv[i], jnp.swapaxes(cache_krope[i], 1, 2), page_table, pages_per_step)
        o_lat = o_lat.reshape(bd, N_HEADS, t, KV_LORA).transpose(0, 2, 1, 3).reshape(n_s, N_HEADS * KV_LORA)
        attn_s = _head_matmul(o_lat, w["wuv_h"], 1, BF16)
        ga_p, hb_p, ulast = _conv_prompt(xp.reshape(b, s, d), w, tm_p)
        state_t = state_conv[i].transpose(1, 0, 2)
        ga_s, hb_s, u_s = _conv_sample(xs.reshape(bd, t, d).transpose(1, 0, 2), state_t, w, _pick_tile(bd, 128))
        from_t = lambda a: a.transpose(1, 0, 2).reshape(n_s, d)
        x1p, base, eidx, ew = _merge((attn_p, attn_s), (ga_p.reshape(n_p, d), from_t(ga_s)),
                                     (hb_p.reshape(n_p, d), from_t(hb_s)), (xp, xs), w, alpha, tm_m)
        rank, counts = _ranks(eidx, tm_n)
        pcounts = (counts[0] + EXPERT_ROWS - 1) // EXPERT_ROWS * EXPERT_ROWS
        pends = jnp.cumsum(pcounts)
        pstart = (pends - pcounts).astype(jnp.int32)
        nused = (pends[-1] // EXPERT_ROWS).astype(jnp.int32)
        blk = jnp.minimum(jnp.arange(n_blocks, dtype=jnp.int32), nused - 1) * EXPERT_ROWS
        blk_e = jnp.minimum(jnp.sum(pends[None, :] <= blk[:, None], axis=1), N_EXPERTS - 1).astype(jnp.int32)
        dest = _destinations(eidx, rank, pstart.astype(F32).reshape(1, N_EXPERTS), tm_n)
        xsorted = _dispatch(dest, x1p, pstart + counts[0], pcounts - counts[0], n_blocks * EXPERT_ROWS, tm_g)
        y = _expert_ffn(blk_e, nused.reshape(1), xsorted, w_eg[i], w_eu[i], w_ed[i])
        p_pair = (p_prompt[i].reshape(n_p, PLE_DIM), p_sample[i].reshape(n_s, PLE_DIM))
        x_all = _combine(dest, y, ew, base, p_pair, w, tm_g)
        xp, xs = x_all[:n_p], x_all[n_p:]
        outs["ckv_p"].append(ckv_p.reshape(b, s, KV_LORA))
        outs["kr_p"].append(kr_p.reshape(b, s, QK_ROPE))
        outs["conv_p"].append(ulast[:, HIST - hist:, :])
        outs["ckv_s"].append(ckv_s.reshape(bd, t, KV_LORA))
        outs["kr_s"].append(kr_s.reshape(bd, t, QK_ROPE))
        outs["conv_s"].append(jnp.concatenate([state_conv[i], u_s.transpose(1, 0, 2)], axis=1)[:, -hist:])
    return (xp.reshape(b, s, d), xs.reshape(bd, t, d), jnp.stack(outs["ckv_p"]), jnp.stack(outs["kr_p"]),
            jnp.stack(outs["conv_p"]), jnp.stack(outs["ckv_s"]), jnp.stack(outs["kr_s"]),
            jnp.stack(outs["conv_s"]))
```

```python
import functools
import math

import jax
import jax.numpy as jnp
from jax import lax
from jax.experimental import pallas as pl
from jax.experimental.pallas import tpu as pltpu

F32 = jnp.float32
BF16 = jnp.bfloat16

D_MODEL = 1024
N_HEADS = 8
QK_NOPE = 128
QK_ROPE = 64
V_HEAD = 128
Q_LORA = 384
KV_LORA = 256
ROPE_THETA = 10000.0
SM_SCALE = (QK_NOPE + QK_ROPE) ** -0.5
CONV_W = 31
N_EXPERTS = 256
TOP_K = 8
N_GROUPS = 8
TOPK_GROUPS = 4
GROUP_SIZE = N_EXPERTS // N_GROUPS
D_EXPERT = D_MODEL // 4
D_SHARED = D_MODEL // 4
ROUTED_SCALE = 2.5
PLE_DIM = 256
LN_EPS = 1e-5
RMS_EPS = 1e-6

LANES = 128
SUBLANES = 8
QK_PAD = 2 * LANES
V_PAD = 2 * LANES
HIST = 32
EXP2_SCALE = SM_SCALE * 1.4426950408889634
ATTN_ROW_CHUNK = 128
EXPERT_ROWS = 512
VMEM_LIMIT = 56 * 1024 * 1024

_NEG_INF = float("-inf")


def _cparams(*sem):
    return pltpu.CompilerParams(dimension_semantics=sem, vmem_limit_bytes=VMEM_LIMIT)


def _const_spec(shape):
    nd = len(shape)
    return pl.BlockSpec(shape, lambda *_: (0,) * nd, pipeline_mode=pl.Buffered(1))


def _dot(a, b):
    return jnp.dot(a, b, preferred_element_type=F32)


def _dot_t(a, b):
    return lax.dot_general(a, b, (((1,), (1,)), ((), ())), preferred_element_type=F32)


def _layernorm(x, g, b):
    mu = jnp.mean(x, axis=-1, keepdims=True)
    xc = x - mu
    var = jnp.mean(xc * xc, axis=-1, keepdims=True)
    return xc * lax.rsqrt(var + LN_EPS) * g + b


def _rmsnorm(x, g):
    return x * lax.rsqrt(jnp.mean(x * x, axis=-1, keepdims=True) + RMS_EPS) * g


def _sigmoid(x):
    return 1.0 / (1.0 + jnp.exp(-x))


def _silu(x):
    return x * _sigmoid(x)


def _attn_proj_kernel(x_ref, cos_ref, sin_ref, wq_ref, wkv_ref, wkr_ref, gq_ref, gkv_ref,
                      wuq_ref, wuqs_ref, wuk_ref, wuv_ref,
                      ckv_ref, kr_ref, q_ref, k_ref, v_ref):
    xb = x_ref[...].astype(BF16)
    cos = cos_ref[...]
    sin = sin_ref[...]
    cq = _rmsnorm(_dot(xb, wq_ref[...]), gq_ref[...]).astype(BF16)
    ckv = _rmsnorm(_dot(xb, wkv_ref[...]), gkv_ref[...])
    ckv_ref[...] = ckv
    ckv_b = ckv.astype(BF16)
    zkr = _dot(xb, wkr_ref[...])
    kr = zkr[:, :LANES] * cos + zkr[:, LANES:] * sin
    kr_ref[...] = kr[:, :QK_ROPE]
    kr_b = kr.astype(BF16)
    qa = _dot(cq, wuq_ref[...])
    qs = _dot(cq, wuqs_ref[...])
    kn = _dot(ckv_b, wuk_ref[...])
    vv = _dot(ckv_b, wuv_ref[...]).astype(BF16)
    ones = jnp.ones((vv.shape[0], V_PAD - V_HEAD), BF16)
    for h in range(N_HEADS):
        v_ref[:, h * V_PAD:h * V_PAD + V_HEAD] = vv[:, h * V_HEAD:(h + 1) * V_HEAD]
        v_ref[:, h * V_PAD + V_HEAD:(h + 1) * V_PAD] = ones
        c0 = h * QK_PAD
        q_ref[:, c0:c0 + LANES] = qa[:, c0:c0 + LANES].astype(BF16)
        q_ref[:, c0 + LANES:c0 + QK_PAD] = (
            qa[:, c0 + LANES:c0 + QK_PAD] * cos + qs[:, h * LANES:(h + 1) * LANES] * sin).astype(BF16)
        k_ref[:, c0:c0 + LANES] = kn[:, h * LANES:(h + 1) * LANES].astype(BF16)
        k_ref[:, c0 + LANES:c0 + QK_PAD] = kr_b


def _attn_proj(x, cos, sin, w, tm):
    n = x.shape[0]
    n_pos = cos.shape[0] // tm
    row = lambda i: (i, 0)
    pos_row = lambda i: (i % n_pos, 0)
    weights = (w["wq"], w["wkv"], w["wkr"], w["g_qn"], w["g_kvn"], w["wuq"], w["wuqs"], w["wuk"], w["wuv"])
    return pl.pallas_call(
        _attn_proj_kernel,
        grid=(n // tm,),
        in_specs=[pl.BlockSpec((tm, D_MODEL), row),
                  pl.BlockSpec((tm, LANES), pos_row),
                  pl.BlockSpec((tm, LANES), pos_row)] + [_const_spec(a.shape) for a in weights],
        out_specs=[pl.BlockSpec((tm, KV_LORA), row),
                   pl.BlockSpec((tm, QK_ROPE), row),
                   pl.BlockSpec((tm, N_HEADS * QK_PAD), row),
                   pl.BlockSpec((tm, N_HEADS * QK_PAD), row),
                   pl.BlockSpec((tm, N_HEADS * V_PAD), row)],
        out_shape=[jax.ShapeDtypeStruct((n, KV_LORA), F32),
                   jax.ShapeDtypeStruct((n, QK_ROPE), F32),
                   jax.ShapeDtypeStruct((n, N_HEADS * QK_PAD), BF16),
                   jax.ShapeDtypeStruct((n, N_HEADS * QK_PAD), BF16),
                   jax.ShapeDtypeStruct((n, N_HEADS * V_PAD), BF16)],
        compiler_params=_cparams("arbitrary"),
        name="attn_proj",
    )(x, cos, sin, *weights)


def _prefill_kernel(q_ref, k_ref, v_ref, o_ref, m_ref, acc_ref, s0_ref, s1_ref, p_ref, *, tq):
    qi = pl.program_id(2)
    q = q_ref[...]
    chunk = min(ATTN_ROW_CHUNK, tq)
    m_ref[...] = jnp.full(m_ref.shape, _NEG_INF, F32)
    acc_ref[...] = jnp.zeros(acc_ref.shape, F32)

    def scores(j, s_ref):
        start = pl.multiple_of(j * tq, tq)
        s_ref[...] = _dot_t(q, k_ref[pl.ds(start, tq), :])

    def consume(j, s_ref, masked):
        start = pl.multiple_of(j * tq, tq)
        for r0 in range(0, tq, chunk):
            rows = slice(r0, r0 + chunk)
            s = s_ref[rows, :] * EXP2_SCALE
            if masked:
                r = lax.broadcasted_iota(jnp.int32, (chunk, tq), 0) + r0
                c = lax.broadcasted_iota(jnp.int32, (chunk, tq), 1)
                s = jnp.where(c <= r, s, _NEG_INF)
            m_old = m_ref[rows, :]
            m_new = jnp.maximum(m_old, jnp.max(s, axis=-1, keepdims=True))
            p_ref[rows, :] = jnp.exp2(s - m_new).astype(BF16)
            acc_ref[rows, :] = acc_ref[rows, :] * jnp.exp2(m_old - m_new)
            m_ref[rows, :] = m_new
        acc_ref[...] += _dot(p_ref[...], v_ref[pl.ds(start, tq), :])

    scores(0, s0_ref)

    def body(i, carry):
        j = 2 * i
        scores(j + 1, s1_ref)
        consume(j, s0_ref, False)
        scores(j + 2, s0_ref)
        consume(j + 1, s1_ref, False)
        return carry

    lax.fori_loop(0, qi // 2, body, 0)

    @pl.when(qi % 2 == 0)
    def _():
        consume(qi, s0_ref, True)

    @pl.when(qi % 2 == 1)
    def _():
        scores(qi, s1_ref)
        consume(qi - 1, s0_ref, False)
        consume(qi, s1_ref, True)

    acc = acc_ref[...]
    o_ref[...] = (acc[:, :V_HEAD] / acc[:, V_HEAD:]).astype(o_ref.dtype)


def _prefill_attention(q, k, v, b, s, tq):
    q3 = q.reshape(b, s, N_HEADS * QK_PAD)
    k3 = k.reshape(b, s, N_HEADS * QK_PAD)
    v3 = v.reshape(b, s, N_HEADS * V_PAD)
    out = pl.pallas_call(
        functools.partial(_prefill_kernel, tq=tq),
        grid=(b, N_HEADS, s // tq),
        in_specs=[pl.BlockSpec((None, tq, QK_PAD), lambda bi, h, i: (bi, i, h)),
                  pl.BlockSpec((None, s, QK_PAD), lambda bi, h, i: (bi, 0, h)),
                  pl.BlockSpec((None, s, V_PAD), lambda bi, h, i: (bi, 0, h))],
        out_specs=pl.BlockSpec((None, tq, V_HEAD), lambda bi, h, i: (bi, i, h)),
        out_shape=jax.ShapeDtypeStruct((b, s, N_HEADS * V_HEAD), BF16),
        scratch_shapes=[pltpu.VMEM((tq, 1), F32), pltpu.VMEM((tq, V_PAD), F32),
                        pltpu.VMEM((tq, tq), F32), pltpu.VMEM((tq, tq), F32), pltpu.VMEM((tq, tq), BF16)],
        compiler_params=_cparams("arbitrary", "arbitrary", "arbitrary"),
        name="prefill_attention",
    )(q3, k3, v3)
    return out.reshape(b * s, N_HEADS * V_HEAD)


def _head_matmul_kernel(a_ref, w_ref, o_ref):
    o_ref[...] = _dot(a_ref[...].astype(BF16), w_ref[...]).astype(o_ref.dtype)


def _head_matmul(a, w, col_stride, out_dtype):
    n = a.shape[0]
    _, kk, mm = w.shape
    return pl.pallas_call(
        _head_matmul_kernel,
        grid=(N_HEADS,),
        in_specs=[pl.BlockSpec((n, kk), lambda h: (0, h * col_stride)),
                  pl.BlockSpec((None, kk, mm), lambda h: (h, 0, 0))],
        out_specs=pl.BlockSpec((n, mm), lambda h: (0, h)),
        out_shape=jax.ShapeDtypeStruct((n, N_HEADS * mm), out_dtype),
        compiler_params=_cparams("arbitrary"),
        name="head_matmul",
    )(a, w)


def _decode_kernel(pt_ref, ql_ref, qr_ref, cn_ref, kn_ref, *rest, pages_per_step, t):
    del pt_ref
    pg = pages_per_step
    ck_refs = rest[:pg]
    kr_refs = rest[pg:2 * pg]
    o_ref, m_ref, l_ref, acc_ref = rest[2 * pg:]
    j = pl.program_id(1)
    ql = ql_ref[...]
    qr = qr_ref[...]
    rows = ql.shape[0]

    @pl.when(j == 0)
    def _():
        cn = cn_ref[...]
        cn_b = cn.astype(BF16)
        s = (_dot_t(ql, cn_b) + _dot_t(qr, kn_ref[...].astype(BF16))) * SM_SCALE
        ti = lax.broadcasted_iota(jnp.int32, (rows, t), 0) % t
        ui = lax.broadcasted_iota(jnp.int32, (rows, t), 1)
        s = jnp.where(ui <= ti, s, _NEG_INF)
        m = jnp.max(s, axis=-1, keepdims=True)
        p = jnp.exp(s - m)
        m_ref[...] = m
        l_ref[...] = jnp.sum(p, axis=-1, keepdims=True)
        acc_ref[...] = _dot(p.astype(BF16), cn_b)

    cks = [r[...].astype(BF16) for r in ck_refs]
    s = jnp.concatenate(
        [_dot_t(ql, ck) + _dot(qr, kr[...].astype(BF16)) for ck, kr in zip(cks, kr_refs)],
        axis=-1) * SM_SCALE
    m_old = m_ref[...]
    m_new = jnp.maximum(m_old, jnp.max(s, axis=-1, keepdims=True))
    corr = jnp.exp(m_old - m_new)
    p = jnp.exp(s - m_new)
    l_ref[...] = l_ref[...] * corr + jnp.sum(p, axis=-1, keepdims=True)
    pb = p.astype(BF16)
    page = cks[0].shape[0]
    pv = _dot(pb[:, :page], cks[0])
    for r in range(1, pg):
        pv = pv + _dot(pb[:, r * page:(r + 1) * page], cks[r])
    acc_ref[...] = acc_ref[...] * corr + pv
    m_ref[...] = m_new

    @pl.when(j == pl.num_programs(1) - 1)
    def _():
        o_ref[...] = acc_ref[...] / l_ref[...]


def _decode_attention(q_lat, q_rope, ckv_new, kr_new, cache_ckv, cache_kr, page_table, pages_per_step):
    bd, rows, _ = q_lat.shape
    t = ckv_new.shape[1]
    n_pages = page_table.shape[1]
    page = cache_ckv.shape[1]
    pg = pages_per_step
    seq = lambda b, j, pt: (b, 0, 0)

    def page_map(r):
        return lambda b, j, pt: (pt[b, j * pg + r], 0, 0)

    grid_spec = pltpu.PrefetchScalarGridSpec(
        num_scalar_prefetch=1,
        grid=(bd, n_pages // pg),
        in_specs=[pl.BlockSpec((None, rows, KV_LORA), seq),
                  pl.BlockSpec((None, rows, QK_ROPE), seq),
                  pl.BlockSpec((None, t, KV_LORA), seq),
                  pl.BlockSpec((None, t, QK_ROPE), seq)]
                 + [pl.BlockSpec((None, page, KV_LORA), page_map(r)) for r in range(pg)]
                 + [pl.BlockSpec((None, QK_ROPE, page), page_map(r)) for r in range(pg)],
        out_specs=pl.BlockSpec((None, rows, KV_LORA), seq),
        scratch_shapes=[pltpu.VMEM((rows, 1), F32), pltpu.VMEM((rows, 1), F32),
                        pltpu.VMEM((rows, KV_LORA), F32)],
    )
    return pl.pallas_call(
        functools.partial(_decode_kernel, pages_per_step=pg, t=t),
        grid_spec=grid_spec,
        out_shape=jax.ShapeDtypeStruct((bd, rows, KV_LORA), F32),
        compiler_params=_cparams("arbitrary", "arbitrary"),
        name="decode_attention",
    )(page_table, q_lat, q_rope, ckv_new, kr_new, *([cache_ckv] * pg), *([cache_kr] * pg))


def _conv_tail(cv, zga, zgb, gcn_ref, bcn_ref, wpb_ref, ga_ref, hb_ref):
    act = _silu(_layernorm(cv, gcn_ref[...], bcn_ref[...])).astype(BF16)
    hb_ref[...] = _sigmoid(zgb) * _dot(act, wpb_ref[...])
    ga_ref[...] = _sigmoid(zga)


def _conv_prompt_kernel(x_ref, wcg_ref, bcg_ref, wdw_ref, bdw_ref, gcn_ref, bcn_ref, wpb_ref,
                        ga_ref, hb_ref, ulast_ref, ush_ref, cv_ref, *, tm, chunk):
    i = pl.program_id(1)
    rows = HIST + tm

    @pl.when(i == 0)
    def _():
        ush_ref[0, 0:HIST, :] = jnp.zeros((HIST, D_MODEL), F32)

    z = _dot(x_ref[...].astype(BF16), wcg_ref[...]) + bcg_ref[...]
    u = z[:, :D_MODEL] * _sigmoid(z[:, D_MODEL:2 * D_MODEL])
    ush_ref[0, HIST:rows, :] = u
    ulast_ref[...] = u[tm - HIST:tm, :]
    for p in range(1, SUBLANES):
        ush_ref[p, 0:rows - SUBLANES, :] = ush_ref[0, p:p + rows - SUBLANES, :]
    off = HIST - (CONV_W - 1)
    for c in range(tm // chunk):
        r0 = c * chunk
        acc = jnp.broadcast_to(bdw_ref[...], (chunk, D_MODEL))
        for jt in range(CONV_W):
            a, p = divmod(off + jt, SUBLANES)
            start = r0 + a * SUBLANES
            acc = acc + wdw_ref[jt:jt + 1, :] * ush_ref[p, start:start + chunk, :]
        cv_ref[r0:r0 + chunk, :] = acc
    ush_ref[0, 0:HIST, :] = ush_ref[0, tm:rows, :]
    _conv_tail(cv_ref[...], z[:, 2 * D_MODEL:3 * D_MODEL], z[:, 3 * D_MODEL:], gcn_ref, bcn_ref, wpb_ref,
               ga_ref, hb_ref)


def _conv_prompt(x3, w, tm):
    b, s, _ = x3.shape
    weights = (w["wcg"], w["b_cg"], w["w_dw"], w["b_dw"], w["g_cn"], w["b_cn"], w["wpb"])
    row = lambda bi, i: (bi, i, 0)
    return pl.pallas_call(
        functools.partial(_conv_prompt_kernel, tm=tm, chunk=32),
        grid=(b, s // tm),
        in_specs=[pl.BlockSpec((None, tm, D_MODEL), row)] + [_const_spec(a.shape) for a in weights],
        out_specs=[pl.BlockSpec((None, tm, D_MODEL), row),
                   pl.BlockSpec((None, tm, D_MODEL), row),
                   pl.BlockSpec((None, HIST, D_MODEL), lambda bi, i: (bi, 0, 0))],
        out_shape=[jax.ShapeDtypeStruct((b, s, D_MODEL), F32),
                   jax.ShapeDtypeStruct((b, s, D_MODEL), F32),
                   jax.ShapeDtypeStruct((b, HIST, D_MODEL), F32)],
        scratch_shapes=[pltpu.VMEM((SUBLANES, HIST + tm, D_MODEL), F32), pltpu.VMEM((tm, D_MODEL), F32)],
        compiler_params=_cparams("arbitrary", "arbitrary"),
        name="conv_prompt",
    )(x3, *weights)


def _conv_sample_kernel(x_ref, st_ref, wcg_ref, bcg_ref, wdw_ref, bdw_ref, gcn_ref, bcn_ref, wpb_ref,
                        ga_ref, hb_ref, u_ref, *, t):
    zs = []
    for ti in range(t):
        z = _dot(x_ref[ti].astype(BF16), wcg_ref[...]) + bcg_ref[...]
        zs.append(z)
        u_ref[ti] = z[:, :D_MODEL] * _sigmoid(z[:, D_MODEL:2 * D_MODEL])
    hist = CONV_W - 1
    for ti in range(t):
        acc = jnp.broadcast_to(bdw_ref[...], u_ref.shape[1:])
        for jt in range(CONV_W):
            p = ti + jt
            src = st_ref[p] if p < hist else u_ref[p - hist]
            acc = acc + wdw_ref[jt:jt + 1, :] * src
        z = zs[ti]
        _conv_tail(acc, z[:, 2 * D_MODEL:3 * D_MODEL], z[:, 3 * D_MODEL:], gcn_ref, bcn_ref, wpb_ref,
                   ga_ref.at[ti], hb_ref.at[ti])


def _conv_sample(x_t, state_t, w, tb):
    t, bd, _ = x_t.shape
    hist = state_t.shape[0]
    weights = (w["wcg"], w["b_cg"], w["w_dw"], w["b_dw"], w["g_cn"], w["b_cn"], w["wpb"])
    blk = lambda i: (0, i, 0)
    return pl.pallas_call(
        functools.partial(_conv_sample_kernel, t=t),
        grid=(bd // tb,),
        in_specs=[pl.BlockSpec((t, tb, D_MODEL), blk), pl.BlockSpec((hist, tb, D_MODEL), blk)]
                 + [_const_spec(a.shape) for a in weights],
        out_specs=[pl.BlockSpec((t, tb, D_MODEL), blk)] * 3,
        out_shape=[jax.ShapeDtypeStruct((t, bd, D_MODEL), F32)] * 3,
        compiler_params=_cparams("arbitrary"),
        name="conv_sample",
    )(x_t, state_t, *weights)


def _first_argmax(vals, lane):
    m = jnp.max(vals, axis=-1, keepdims=True)
    idx = jnp.min(jnp.where(vals == m, lane, float(vals.shape[-1])), axis=-1, keepdims=True)
    return m, idx


def _route(scores, bias, eidx_ref, ew_ref):
    tm = scores.shape[0]
    lane = lax.broadcasted_iota(jnp.int32, (tm, N_EXPERTS), 1).astype(F32)
    group = jnp.floor(lane * (1.0 / GROUP_SIZE))
    sel = scores + bias
    gscore = []
    for g in range(N_GROUPS):
        vals = jnp.where(group == g, sel, _NEG_INF)
        m1, i1 = _first_argmax(vals, lane)
        m2 = jnp.max(jnp.where(lane == i1, _NEG_INF, vals), axis=-1, keepdims=True)
        gscore.append(m1 + m2)
    vals = jnp.full((tm, N_EXPERTS), _NEG_INF, F32)
    for g in range(N_GROUPS):
        beaten = jnp.zeros((tm, 1), F32)
        for o in range(N_GROUPS):
            if o != g:
                ahead = gscore[o] >= gscore[g] if o < g else gscore[o] > gscore[g]
                beaten = beaten + jnp.where(ahead, 1.0, 0.0)
        kept = jnp.where(beaten < TOPK_GROUPS, 1.0, 0.0)
        vals = jnp.where((group == g) & (kept > 0.5), sel, vals)
    out_lane = lax.broadcasted_iota(jnp.int32, (tm, TOP_K), 1)
    eidx = jnp.zeros((tm, TOP_K), F32)
    ew = jnp.zeros((tm, TOP_K), F32)
    for k in range(TOP_K):
        _, ik = _first_argmax(vals, lane)
        hit = lane == ik
        wk = jnp.sum(jnp.where(hit, scores, 0.0), axis=-1, keepdims=True)
        vals = jnp.where(hit, _NEG_INF, vals)
        eidx = jnp.where(out_lane == k, ik, eidx)
        ew = jnp.where(out_lane == k, wk, ew)
    eidx_ref[...] = eidx.astype(jnp.int32)
    ew_ref[...] = ew / jnp.sum(ew, axis=-1, keepdims=True) * ROUTED_SCALE


_HI_MASK = 0xFFFF0000


def _pack_bf16_pair(x):
    half = x.shape[1] // 2
    lo = lax.bitcast_convert_type(x[:, :half].astype(BF16).astype(F32), jnp.uint32) >> 16
    hi = lax.bitcast_convert_type(x[:, half:].astype(BF16).astype(F32), jnp.uint32) & jnp.uint32(_HI_MASK)
    return lo | hi


def _unpack_bf16_pair(u):
    lo = lax.bitcast_convert_type(u << 16, F32).astype(BF16)
    hi = lax.bitcast_convert_type(u & jnp.uint32(_HI_MASK), F32).astype(BF16)
    return lo, hi


def _prompt_or_sample(p_ref, s_ref, is_sample):
    return jnp.where(is_sample, s_ref[...], p_ref[...])


def _merge_kernel(attn_p, attn_s, ga_p, ga_s, hb_p, hb_s, x_p, x_s, wpa_ref, wo_ref, g1_ref, b1_ref, wsgu_ref,
                  wsd_ref, wr_ref, br_ref, x1p_ref, base_ref, eidx_ref, ew_ref, *, alpha, prompt_tiles):
    is_sample = pl.program_id(0) >= prompt_tiles
    attn = _prompt_or_sample(attn_p, attn_s, is_sample)
    h = _prompt_or_sample(ga_p, ga_s, is_sample) * _dot(attn, wpa_ref[...]) + _prompt_or_sample(hb_p, hb_s, is_sample)
    x = _prompt_or_sample(x_p, x_s, is_sample)
    x1 = _layernorm(alpha * x + _dot(h.astype(BF16), wo_ref[...]), g1_ref[...], b1_ref[...])
    x1p_ref[...] = _pack_bf16_pair(x1)
    x1b = x1.astype(BF16)
    gu = _dot(x1b, wsgu_ref[...])
    mid = (_silu(gu[:, :D_SHARED]) * gu[:, D_SHARED:]).astype(BF16)
    base_ref[...] = alpha * x1 + _dot(mid, wsd_ref[...])
    scores = _sigmoid(_dot(x1b, wr_ref[...]))
    _route(scores, br_ref[...], eidx_ref, ew_ref)


def _split_specs(block, prompt_tiles, shift=0):
    nd = len(block)
    rest = (0,) * (nd - 1)
    tile = lambda i: jnp.maximum(i - shift, 0)
    return [pl.BlockSpec(block, lambda i: (jnp.minimum(tile(i), prompt_tiles - 1),) + rest),
            pl.BlockSpec(block, lambda i: (jnp.maximum(tile(i) - prompt_tiles, 0),) + rest)]


def _merge(attn, ga, hb, x, w, alpha, tm):
    n_p, n_s = x[0].shape[0], x[1].shape[0]
    n = n_p + n_s
    prompt_tiles = n_p // tm
    row = lambda i: (i, 0)
    weights = (w["wpa"], w["wo"], w["g_ln1"], w["b_ln1"], w["wsgu"], w["wsd"], w["wr"], w["b_router"])
    small = pl.BlockSpec((tm, TOP_K), row)
    return pl.pallas_call(
        functools.partial(_merge_kernel, alpha=alpha, prompt_tiles=prompt_tiles),
        grid=(n // tm,),
        in_specs=_split_specs((tm, D_MODEL), prompt_tiles) * 4 + [_const_spec(a.shape) for a in weights],
        out_specs=[pl.BlockSpec((tm, D_MODEL // 2), row), pl.BlockSpec((tm, D_MODEL), row), small, small],
        out_shape=[jax.ShapeDtypeStruct((n, D_MODEL // 2), jnp.uint32), jax.ShapeDtypeStruct((n, D_MODEL), F32),
                   jax.ShapeDtypeStruct((n, TOP_K), jnp.int32), jax.ShapeDtypeStruct((n, TOP_K), F32)],
        compiler_params=_cparams("arbitrary"),
        name="merge_route",
    )(*attn, *ga, *hb, *x, *weights)


def _onehots(eidx, tm):
    lane = lax.broadcasted_iota(jnp.int32, (tm, N_EXPERTS), 1)
    return [lane == eidx[:, k:k + 1] for k in range(TOP_K)]


def _rank_kernel(eidx_ref, rank_ref, count_ref, carry_ref, *, tm):
    @pl.when(pl.program_id(0) == 0)
    def _():
        carry_ref[...] = jnp.zeros(carry_ref.shape, F32)

    hits = _onehots(eidx_ref[...], tm)
    member = hits[0]
    for hk in hits[1:]:
        member = member | hk
    member_b = jnp.where(member, 1.0, 0.0).astype(BF16)
    r = lax.broadcasted_iota(jnp.int32, (tm, tm), 0)
    c = lax.broadcasted_iota(jnp.int32, (tm, tm), 1)
    below = jnp.where(c < r, 1.0, 0.0).astype(BF16)
    before = _dot(below, member_b) + carry_ref[...]
    out_lane = lax.broadcasted_iota(jnp.int32, (tm, TOP_K), 1)
    rank = jnp.zeros((tm, TOP_K), F32)
    for k in range(TOP_K):
        rk = jnp.sum(jnp.where(hits[k], before, 0.0), axis=-1, keepdims=True)
        rank = jnp.where(out_lane == k, rk, rank)
    rank_ref[...] = rank.astype(jnp.int32)
    total = carry_ref[...] + jnp.sum(member_b.astype(F32), axis=0, keepdims=True)
    carry_ref[...] = total
    count_ref[...] = total.astype(jnp.int32)


def _ranks(eidx, tm):
    n = eidx.shape[0]
    return pl.pallas_call(
        functools.partial(_rank_kernel, tm=tm),
        grid=(n // tm,),
        in_specs=[pl.BlockSpec((tm, TOP_K), lambda i: (i, 0))],
        out_specs=[pl.BlockSpec((tm, TOP_K), lambda i: (i, 0)), _const_spec((1, N_EXPERTS))],
        out_shape=[jax.ShapeDtypeStruct((n, TOP_K), jnp.int32),
                   jax.ShapeDtypeStruct((1, N_EXPERTS), jnp.int32)],
        scratch_shapes=[pltpu.VMEM((1, N_EXPERTS), F32)],
        compiler_params=_cparams("arbitrary"),
        name="expert_ranks",
    )(eidx)


def _dest_kernel(eidx_ref, rank_ref, pstart_ref, dest_ref, *, tm):
    hits = _onehots(eidx_ref[...], tm)
    pstart = pstart_ref[...]
    out_lane = lax.broadcasted_iota(jnp.int32, (tm, TOP_K), 1)
    base = jnp.zeros((tm, TOP_K), F32)
    for k in range(TOP_K):
        bk = jnp.sum(jnp.where(hits[k], pstart, 0.0), axis=-1, keepdims=True)
        base = jnp.where(out_lane == k, bk, base)
    dest_ref[...] = base.astype(jnp.int32) + rank_ref[...]


def _destinations(eidx, rank, pstart, tm):
    n = eidx.shape[0]
    row = pl.BlockSpec((tm, TOP_K), lambda i: (i, 0))
    return pl.pallas_call(
        functools.partial(_dest_kernel, tm=tm),
        grid=(n // tm,),
        in_specs=[row, row, _const_spec((1, N_EXPERTS))],
        out_specs=row,
        out_shape=jax.ShapeDtypeStruct((n, TOP_K), jnp.int32),
        compiler_params=_cparams("arbitrary"),
        name="expert_dest",
    )(eidx, rank, pstart)


def _row_copy(src_ref, src_row, dst_ref, dst_row, sem):
    return pltpu.make_async_copy(src_ref.at[pl.ds(src_row, 1), :], dst_ref.at[pl.ds(dst_row, 1), :], sem)


_PAD_PIECES = tuple(EXPERT_ROWS >> (bit + 1) for bit in range(EXPERT_ROWS.bit_length() - 1)
                    if EXPERT_ROWS >> (bit + 1) >= SUBLANES)


def _zero_fill(pad_start_ref, pad_count_ref, zero_ref, xs_hbm, sem, wait):
    def go(cp):
        if wait:
            cp.wait()
        else:
            cp.start()

    def per_expert(e, carry):
        start = pad_start_ref[e]
        lead = (-start) & (SUBLANES - 1)
        for r in range(SUBLANES - 1):
            @pl.when(r < lead)
            def _(r=r):
                go(_row_copy(zero_ref, 0, xs_hbm, start + r, sem))

        rest = pad_count_ref[e] - lead
        pos = start + lead
        for size in _PAD_PIECES:
            has = (rest & size) != 0

            @pl.when(has)
            def _(pos=pos, size=size):
                dst = xs_hbm.at[pl.ds(pl.multiple_of(pos, SUBLANES), size), :]
                go(pltpu.make_async_copy(zero_ref.at[pl.ds(0, size), :], dst, sem))

            pos = pos + jnp.where(has, size, 0)
        return carry

    lax.fori_loop(0, N_EXPERTS, per_expert, 0)

    piece = zero_ref.shape[0]

    def tail(j, carry):
        dst = xs_hbm.at[pl.ds(pl.multiple_of(j * piece, piece), piece), :]
        go(pltpu.make_async_copy(zero_ref, dst, sem))
        return carry

    end = pad_start_ref[N_EXPERTS - 1] + pad_count_ref[N_EXPERTS - 1]
    lax.fori_loop(end // piece, xs_hbm.shape[0] // piece, tail, 0)


def _wait_row_copies(src_ref, dst_ref, sem, count):
    def body(r, carry):
        for _ in range(TOP_K):
            _row_copy(src_ref, 0, dst_ref, 0, sem).wait()
        return carry

    lax.fori_loop(0, count, body, 0)


def _dispatch_kernel(pad_start_ref, pad_count_ref, dest_hbm, x_ref, xs_hbm, dest_smem, zero_ref, isem, sem,
                     zsem, *, tm):
    i = pl.program_id(0)
    last = pl.num_programs(0) - 1
    per_tile = tm * TOP_K
    slot = i % 2

    def idx_copy(step, s):
        dst = dest_smem.at[pl.ds(pl.multiple_of(s * per_tile, per_tile), per_tile)]
        return pltpu.make_async_copy(dest_hbm.at[step], dst, isem.at[s])

    @pl.when(i == 0)
    def _():
        zero_ref[...] = jnp.zeros(zero_ref.shape, zero_ref.dtype)
        _zero_fill(pad_start_ref, pad_count_ref, zero_ref, xs_hbm, zsem, wait=False)
        idx_copy(0, 0).start()

    @pl.when(i < last)
    def _():
        idx_copy(i + 1, 1 - slot).start()

    idx_copy(i, slot).wait()
    first = slot * per_tile

    def issue(r, carry):
        for k in range(TOP_K):
            _row_copy(x_ref, r, xs_hbm, dest_smem[first + r * TOP_K + k], sem).start()
        return carry

    lax.fori_loop(0, tm, issue, 0)
    _wait_row_copies(x_ref, xs_hbm, sem, tm)

    @pl.when(i == last)
    def _():
        _zero_fill(pad_start_ref, pad_count_ref, zero_ref, xs_hbm, zsem, wait=True)


def _dispatch(dest, x1p, pad_start, pad_count, rows, tm):
    n, width = x1p.shape
    dest2 = dest.reshape(n // tm, tm * TOP_K)
    grid_spec = pltpu.PrefetchScalarGridSpec(
        num_scalar_prefetch=2,
        grid=(n // tm,),
        in_specs=[pl.BlockSpec(memory_space=pl.ANY),
                  pl.BlockSpec((tm, width), lambda i, ps, pc: (i, 0))],
        out_specs=pl.BlockSpec(memory_space=pl.ANY),
        scratch_shapes=[pltpu.SMEM((2 * tm * TOP_K,), jnp.int32), pltpu.VMEM((_PAD_PIECES[0], width), x1p.dtype),
                        pltpu.SemaphoreType.DMA((2,)), pltpu.SemaphoreType.DMA, pltpu.SemaphoreType.DMA],
    )
    return pl.pallas_call(
        functools.partial(_dispatch_kernel, tm=tm),
        grid_spec=grid_spec,
        out_shape=jax.ShapeDtypeStruct((rows, width), x1p.dtype),
        compiler_params=_cparams("arbitrary"),
        name="dispatch",
    )(pad_start, pad_count, dest2, x1p)


def _expert_kernel(blk_e_ref, nused_ref, xs_ref, wg_ref, wu_ref, wd_ref, y_ref):
    del blk_e_ref
    g = pl.program_id(0)

    @pl.when(g < nused_ref[0])
    def _():
        lo, hi = _unpack_bf16_pair(xs_ref[...])
        half = D_MODEL // 2

        def proj(w_ref):
            return _dot(lo, w_ref[:half, :].astype(BF16)) + _dot(hi, w_ref[half:, :].astype(BF16))

        mid = (_silu(proj(wg_ref)) * proj(wu_ref)).astype(BF16)
        y_ref[...] = _dot(mid, wd_ref[...].astype(BF16))

    @pl.when(g >= nused_ref[0])
    def _():
        y_ref[...] = jnp.zeros(y_ref.shape, F32)


def _expert_ffn(blk_e, nused, xs, w_eg, w_eu, w_ed):
    rows = xs.shape[0]
    n_blocks = rows // EXPERT_ROWS
    row = lambda g, be, nu: (g, 0)
    in_row = lambda g, be, nu: (jnp.minimum(g, nu[0] - 1), 0)
    wsel = lambda g, be, nu: (be[g], 0, 0)
    grid_spec = pltpu.PrefetchScalarGridSpec(
        num_scalar_prefetch=2,
        grid=(n_blocks,),
        in_specs=[pl.BlockSpec((EXPERT_ROWS, xs.shape[1]), in_row),
                  pl.BlockSpec((None, D_MODEL, D_EXPERT), wsel),
                  pl.BlockSpec((None, D_MODEL, D_EXPERT), wsel),
                  pl.BlockSpec((None, D_EXPERT, D_MODEL), wsel)],
        out_specs=pl.BlockSpec((EXPERT_ROWS, D_MODEL), row),
    )
    return pl.pallas_call(
        _expert_kernel,
        grid_spec=grid_spec,
        out_shape=jax.ShapeDtypeStruct((rows, D_MODEL), F32),
        compiler_params=_cparams("arbitrary"),
        name="expert_ffn",
    )(blk_e, nused, xs, w_eg, w_eu, w_ed)


def _combine_kernel(dest_hbm, y_hbm, ew_ref, base_ref, pp_ref, ps_ref, g2_ref, b2_ref, wple_ref, wpleg_ref,
                    out_ref, dest_smem, ybuf_ref, isem, sem, *, tm, prompt_tiles):
    j = pl.program_id(0)
    tiles = pl.num_programs(0) - 1
    per_tile = tm * TOP_K

    def idx_copy(tile):
        s = tile % 2
        dst = dest_smem.at[pl.ds(pl.multiple_of(s * per_tile, per_tile), per_tile)]
        return pltpu.make_async_copy(dest_hbm.at[tile], dst, isem.at[s])

    @pl.when(j == 0)
    def _():
        idx_copy(0).start()

    @pl.when(j + 1 < tiles)
    def _():
        idx_copy(j + 1).start()

    @pl.when(j < tiles)
    def _():
        idx_copy(j).wait()
        s = j % 2
        first = s * per_tile

        def issue(r, carry):
            for k in range(TOP_K):
                _row_copy(y_hbm, dest_smem[first + r * TOP_K + k], ybuf_ref.at[s, k], r, sem.at[s]).start()
            return carry

        lax.fori_loop(0, tm, issue, 0)

    @pl.when(j > 0)
    def _():
        c = j - 1
        s = c % 2
        _wait_row_copies(y_hbm, ybuf_ref.at[s, 0], sem.at[s], tm)
        ew = ew_ref[...]
        routed = ew[:, 0:1] * ybuf_ref[s, 0]
        for k in range(1, TOP_K):
            routed = routed + ew[:, k:k + 1] * ybuf_ref[s, k]
        x2 = _layernorm(base_ref[...] + routed, g2_ref[...], b2_ref[...])
        p = _prompt_or_sample(pp_ref, ps_ref, c >= prompt_tiles)
        emb = _dot(p.astype(BF16), wple_ref[...])
        out_ref[...] = x2 + emb * _sigmoid(_dot(x2.astype(BF16), wpleg_ref[...]))


def _combine(dest, y, ew, base, p, w, tm):
    n = base.shape[0]
    prompt_tiles = p[0].shape[0] // tm
    dest2 = dest.reshape(n // tm, tm * TOP_K)
    row = lambda j: (jnp.maximum(j - 1, 0), 0)
    weights = (w["g_ln2"], w["b_ln2"], w["wple"], w["wpleg"])
    return pl.pallas_call(
        functools.partial(_combine_kernel, tm=tm, prompt_tiles=prompt_tiles),
        grid=(n // tm + 1,),
        in_specs=[pl.BlockSpec(memory_space=pl.ANY), pl.BlockSpec(memory_space=pl.ANY),
                  pl.BlockSpec((tm, TOP_K), row), pl.BlockSpec((tm, D_MODEL), row)]
                 + _split_specs((tm, PLE_DIM), prompt_tiles, shift=1) + [_const_spec(a.shape) for a in weights],
        out_specs=pl.BlockSpec((tm, D_MODEL), row),
        out_shape=jax.ShapeDtypeStruct((n, D_MODEL), F32),
        scratch_shapes=[pltpu.SMEM((2 * tm * TOP_K,), jnp.int32), pltpu.VMEM((2, TOP_K, tm, D_MODEL), F32),
                        pltpu.SemaphoreType.DMA((2,)), pltpu.SemaphoreType.DMA((2,))],
        compiler_params=_cparams("arbitrary"),
        name="combine",
    )(dest2, y, ew, base, *p, *weights)


def _rope_tables(pos):
    half = QK_ROPE // 2
    inv = ROPE_THETA ** (-jnp.arange(half, dtype=F32) / half)
    ang = pos.astype(F32)[:, None] * inv[None, :]
    reps = LANES // half
    return jnp.tile(jnp.cos(ang), (1, reps)), jnp.tile(jnp.sin(ang), (1, reps))


def _swap_halves(w):
    half = w.shape[-1] // 2
    return jnp.concatenate([-w[..., half:], w[..., :half]], axis=-1)


def _layer_weights(i, w_in, b_cg, g_qn, w_uq, g_kvn, w_uk, w_uv, w_pa, w_dw, b_dw, g_cn, b_cn, w_pb, w_o,
                   g_ln1, b_ln1, w_router, b_router, w_sg, w_su, w_sd, g_ln2, b_ln2, w_ple, w_pleg):
    o1, o2, o3 = Q_LORA, Q_LORA + KV_LORA, Q_LORA + KV_LORA + QK_ROPE
    win = w_in[i]
    row = lambda v: v[i].reshape(1, -1)
    wkr = win[:, o2:o3]
    zpad = jnp.zeros((D_MODEL, LANES - QK_ROPE), F32)
    uq = w_uq[i].reshape(Q_LORA, N_HEADS, QK_NOPE + QK_ROPE)
    uq_n, uq_r = uq[..., :QK_NOPE], uq[..., QK_NOPE:]
    hpad = jnp.zeros((Q_LORA, N_HEADS, LANES - QK_ROPE), F32)
    return {
        "wq": win[:, :o1].astype(BF16),
        "wkv": win[:, o1:o2].astype(BF16),
        "wkr": jnp.concatenate([wkr, zpad, _swap_halves(wkr), zpad], axis=1).astype(BF16),
        "wcg": win[:, o3:].astype(BF16),
        "b_cg": row(b_cg),
        "g_qn": row(g_qn),
        "g_kvn": row(g_kvn),
        "wuq": jnp.concatenate([uq_n, uq_r, hpad], axis=-1).reshape(Q_LORA, N_HEADS * QK_PAD).astype(BF16),
        "wuqs": jnp.concatenate([_swap_halves(uq_r), hpad], axis=-1).reshape(Q_LORA, N_HEADS * LANES).astype(BF16),
        "wuk": w_uk[i].reshape(KV_LORA, N_HEADS * QK_NOPE).astype(BF16),
        "wuv": w_uv[i].reshape(KV_LORA, N_HEADS * V_HEAD).astype(BF16),
        "wuk_t": w_uk[i].transpose(1, 2, 0).astype(BF16),
        "wuv_h": w_uv[i].transpose(1, 0, 2).astype(BF16),
        "wpa": w_pa[i].astype(BF16),
        "w_dw": w_dw[i],
        "b_dw": row(b_dw),
        "g_cn": row(g_cn),
        "b_cn": row(b_cn),
        "wpb": w_pb[i].astype(BF16),
        "wo": w_o[i].astype(BF16),
        "g_ln1": row(g_ln1),
        "b_ln1": row(b_ln1),
        "wr": w_router[i].astype(BF16),
        "b_router": row(b_router),
        "wsgu": jnp.concatenate([w_sg[i], w_su[i]], axis=1).astype(BF16),
        "wsd": w_sd[i].astype(BF16),
        "g_ln2": row(g_ln2),
        "b_ln2": row(b_ln2),
        "wple": w_ple[i].astype(BF16),
        "wpleg": w_pleg[i].astype(BF16),
    }


def _pick_tile(n, target):
    tm = min(n, target)
    while n % tm:
        tm //= 2
    return tm


def kernel(x_prompt, x_sample, cache_ckv, cache_krope, state_conv, page_table, p_prompt, p_sample, w_in, b_cg, g_qn, w_uq, g_kvn, w_uk, w_uv, w_pa, w_dw, b_dw, g_cn, b_cn, w_pb, w_o, g_ln1, b_ln1, w_router, b_router, w_eg, w_eu, w_ed, w_sg, w_su, w_sd, g_ln2, b_ln2, w_ple, w_pleg):
    b, s, d = x_prompt.shape
    bd, t, _ = x_sample.shape
    depth = w_in.shape[0]
    n_p, n_s = b * s, bd * t
    n = n_p + n_s
    past = page_table.shape[1] * cache_ckv.shape[2]
    alpha = (2 * depth) ** 0.25
    hist = CONV_W - 1

    cos_p, sin_p = _rope_tables(jnp.arange(s, dtype=jnp.int32))
    cos_s, sin_s = _rope_tables(past + jnp.tile(jnp.arange(t, dtype=jnp.int32), bd))

    tm_p = _pick_tile(s, 256)
    tm_s = _pick_tile(n_s, 256)
    common = math.gcd(n_p, n_s)
    tm_n = _pick_tile(common, 512)
    tm_m = _pick_tile(common, 512)
    tm_g = _pick_tile(common, 128)
    tq = _pick_tile(s, 512)
    pages_per_step = _pick_tile(page_table.shape[1], 16)
    n_blocks = -(-(n * TOP_K + N_EXPERTS * (EXPERT_ROWS - 1)) // EXPERT_ROWS)

    xp = x_prompt.reshape(n_p, d)
    xs = x_sample.reshape(n_s, d)
    outs = {k: [] for k in ("ckv_p", "kr_p", "conv_p", "ckv_s", "kr_s", "conv_s")}
    for i in range(depth):
        w = _layer_weights(i, w_in, b_cg, g_qn, w_uq, g_kvn, w_uk, w_uv, w_pa, w_dw, b_dw, g_cn, b_cn, w_pb,
                           w_o, g_ln1, b_ln1, w_router, b_router, w_sg, w_su, w_sd, g_ln2, b_ln2, w_ple, w_pleg)
        ckv_p, kr_p, q_p, k_p, v_p = _attn_proj(xp, cos_p, sin_p, w, tm_p)
        attn_p = _prefill_attention(q_p, k_p, v_p, b, s, tq)
        ckv_s, kr_s, q_s, _, _ = _attn_proj(xs, cos_s, sin_s, w, tm_s)
        q_lat = _head_matmul(q_s, w["wuk_t"], 2, BF16)
        q_lat = q_lat.reshape(bd, t, N_HEADS, KV_LORA).transpose(0, 2, 1, 3).reshape(bd, N_HEADS * t, KV_LORA)
        q_r = q_s.reshape(bd, t, N_HEADS, QK_PAD)[..., QK_NOPE:QK_NOPE + QK_ROPE]
        q_r = q_r.transpose(0, 2, 1, 3).reshape(bd, N_HEADS * t, QK_ROPE)
        o_lat = _decode_attention(q_lat, q_r, ckv_s.reshape(bd, t, KV_LORA), kr_s.reshape(bd, t, QK_ROPE),
                                  cache_ckv[i], jnp.swapaxes(cache_krope[i], 1, 2), page_table, pages_per_step)
        o_lat = o_lat.reshape(bd, N_HEADS, t, KV_LORA).transpose(0, 2, 1, 3).reshape(n_s, N_HEADS * KV_LORA)
        attn_s = _head_matmul(o_lat, w["wuv_h"], 1, BF16)
        ga_p, hb_p, ulast = _conv_prompt(xp.reshape(b, s, d), w, tm_p)
        state_t = state_conv[i].transpose(1, 0, 2)
        ga_s, hb_s, u_s = _conv_sample(xs.reshape(bd, t, d).transpose(1, 0, 2), state_t, w, _pick_tile(bd, 128))
        from_t = lambda a: a.transpose(1, 0, 2).reshape(n_s, d)
        x1p, base, eidx, ew = _merge((attn_p, attn_s), (ga_p.reshape(n_p, d), from_t(ga_s)),
                                     (hb_p.reshape(n_p, d), from_t(hb_s)), (xp, xs), w, alpha, tm_m)
        rank, counts = _ranks(eidx, tm_n)
        pcounts = (counts[0] + EXPERT_ROWS - 1) // EXPERT_ROWS * EXPERT_ROWS
        pends = jnp.cumsum(pcounts)
        pstart = (pends - pcounts).astype(jnp.int32)
        nused = (pends[-1] // EXPERT_ROWS).astype(jnp.int32)
        blk = jnp.minimum(jnp.arange(n_blocks, dtype=jnp.int32), nused - 1) * EXPERT_ROWS
        blk_e = jnp.minimum(jnp.sum(pends[None, :] <= blk[:, None], axis=1), N_EXPERTS - 1).astype(jnp.int32)
        dest = _destinations(eidx, rank, pstart.astype(F32).reshape(1, N_EXPERTS), tm_n)
        xsorted = _dispatch(dest, x1p, pstart + counts[0], pcounts - counts[0], n_blocks * EXPERT_ROWS, tm_g)
        y = _expert_ffn(blk_e, nused.reshape(1), xsorted, w_eg[i], w_eu[i], w_ed[i])
        p_pair = (p_prompt[i].reshape(n_p, PLE_DIM), p_sample[i].reshape(n_s, PLE_DIM))
        x_all = _combine(dest, y, ew, base, p_pair, w, tm_g)
        xp, xs = x_all[:n_p], x_all[n_p:]
        outs["ckv_p"].append(ckv_p.reshape(b, s, KV_LORA))
        outs["kr_p"].append(kr_p.reshape(b, s, QK_ROPE))
        outs["conv_p"].append(ulast[:, HIST - hist:, :])
        outs["ckv_s"].append(ckv_s.reshape(bd, t, KV_LORA))
        outs["kr_s"].append(kr_s.reshape(bd, t, QK_ROPE))
        outs["conv_s"].append(jnp.concatenate([state_conv[i], u_s.transpose(1, 0, 2)], axis=1)[:, -hist:])
    return (xp.reshape(b, s, d), xs.reshape(bd, t, d), jnp.stack(outs["ckv_p"]), jnp.stack(outs["kr_p"]),
            jnp.stack(outs["conv_p"]), jnp.stack(outs["ckv_s"]), jnp.stack(outs["kr_s"]),
            jnp.stack(outs["conv_s"]))
```

```python
import functools
import math

import jax
import jax.numpy as jnp
from jax import lax
from jax.experimental import pallas as pl
from jax.experimental.pallas import tpu as pltpu

F32 = jnp.float32
BF16 = jnp.bfloat16

D_MODEL = 1024
N_HEADS = 8
QK_NOPE = 128
QK_ROPE = 64
V_HEAD = 128
Q_LORA = 384
KV_LORA = 256
ROPE_THETA = 10000.0
SM_SCALE = (QK_NOPE + QK_ROPE) ** -0.5
CONV_W = 31
N_EXPERTS = 256
TOP_K = 8
N_GROUPS = 8
TOPK_GROUPS = 4
GROUP_SIZE = N_EXPERTS // N_GROUPS
D_EXPERT = D_MODEL // 4
D_SHARED = D_MODEL // 4
ROUTED_SCALE = 2.5
PLE_DIM = 256
LN_EPS = 1e-5
RMS_EPS = 1e-6

LANES = 128
SUBLANES = 8
QK_PAD = 2 * LANES
V_PAD = 2 * LANES
HIST = 32
EXP2_SCALE = SM_SCALE * 1.4426950408889634
ATTN_ROW_CHUNK = 128
EXPERT_ROWS = 512
VMEM_LIMIT = 56 * 1024 * 1024

_NEG_INF = float("-inf")


def _cparams(*sem):
    return pltpu.CompilerParams(dimension_semantics=sem, vmem_limit_bytes=VMEM_LIMIT)


def _const_spec(shape):
    nd = len(shape)
    return pl.BlockSpec(shape, lambda *_: (0,) * nd, pipeline_mode=pl.Buffered(1))


def _dot(a, b):
    return jnp.dot(a, b, preferred_element_type=F32)


def _dot_t(a, b):
    return lax.dot_general(a, b, (((1,), (1,)), ((), ())), preferred_element_type=F32)


def _layernorm(x, g, b):
    mu = jnp.mean(x, axis=-1, keepdims=True)
    xc = x - mu
    var = jnp.mean(xc * xc, axis=-1, keepdims=True)
    return xc * lax.rsqrt(var + LN_EPS) * g + b


def _rmsnorm(x, g):
    return x * lax.rsqrt(jnp.mean(x * x, axis=-1, keepdims=True) + RMS_EPS) * g


def _sigmoid(x):
    return 1.0 / (1.0 + jnp.exp(-x))


def _silu(x):
    return x * _sigmoid(x)


def _attn_proj_kernel(x_ref, cos_ref, sin_ref, wq_ref, wkv_ref, wkr_ref, gq_ref, gkv_ref,
                      wuq_ref, wuqs_ref, wuk_ref, wuv_ref,
                      ckv_ref, kr_ref, q_ref, k_ref, v_ref):
    xb = x_ref[...].astype(BF16)
    cos = cos_ref[...]
    sin = sin_ref[...]
    cq = _rmsnorm(_dot(xb, wq_ref[...]), gq_ref[...]).astype(BF16)
    ckv = _rmsnorm(_dot(xb, wkv_ref[...]), gkv_ref[...])
    ckv_ref[...] = ckv
    ckv_b = ckv.astype(BF16)
    zkr = _dot(xb, wkr_ref[...])
    kr = zkr[:, :LANES] * cos + zkr[:, LANES:] * sin
    kr_ref[...] = kr[:, :QK_ROPE]
    kr_b = kr.astype(BF16)
    qa = _dot(cq, wuq_ref[...])
    qs = _dot(cq, wuqs_ref[...])
    kn = _dot(ckv_b, wuk_ref[...])
    vv = _dot(ckv_b, wuv_ref[...]).astype(BF16)
    ones = jnp.ones((vv.shape[0], V_PAD - V_HEAD), BF16)
    for h in range(N_HEADS):
        v_ref[:, h * V_PAD:h * V_PAD + V_HEAD] = vv[:, h * V_HEAD:(h + 1) * V_HEAD]
        v_ref[:, h * V_PAD + V_HEAD:(h + 1) * V_PAD] = ones
        c0 = h * QK_PAD
        q_ref[:, c0:c0 + LANES] = qa[:, c0:c0 + LANES].astype(BF16)
        q_ref[:, c0 + LANES:c0 + QK_PAD] = (
            qa[:, c0 + LANES:c0 + QK_PAD] * cos + qs[:, h * LANES:(h + 1) * LANES] * sin).astype(BF16)
        k_ref[:, c0:c0 + LANES] = kn[:, h * LANES:(h + 1) * LANES].astype(BF16)
        k_ref[:, c0 + LANES:c0 + QK_PAD] = kr_b


def _attn_proj(x, cos, sin, w, tm):
    n = x.shape[0]
    n_pos = cos.shape[0] // tm
    row = lambda i: (i, 0)
    pos_row = lambda i: (i % n_pos, 0)
    weights = (w["wq"], w["wkv"], w["wkr"], w["g_qn"], w["g_kvn"], w["wuq"], w["wuqs"], w["wuk"], w["wuv"])
    return pl.pallas_call(
        _attn_proj_kernel,
        grid=(n // tm,),
        in_specs=[pl.BlockSpec((tm, D_MODEL), row),
                  pl.BlockSpec((tm, LANES), pos_row),
                  pl.BlockSpec((tm, LANES), pos_row)] + [_const_spec(a.shape) for a in weights],
        out_specs=[pl.BlockSpec((tm, KV_LORA), row),
                   pl.BlockSpec((tm, QK_ROPE), row),
                   pl.BlockSpec((tm, N_HEADS * QK_PAD), row),
                   pl.BlockSpec((tm, N_HEADS * QK_PAD), row),
                   pl.BlockSpec((tm, N_HEADS * V_PAD), row)],
        out_shape=[jax.ShapeDtypeStruct((n, KV_LORA), F32),
                   jax.ShapeDtypeStruct((n, QK_ROPE), F32),
                   jax.ShapeDtypeStruct((n, N_HEADS * QK_PAD), BF16),
                   jax.ShapeDtypeStruct((n, N_HEADS * QK_PAD), BF16),
                   jax.ShapeDtypeStruct((n, N_HEADS * V_PAD), BF16)],
        compiler_params=_cparams("arbitrary"),
        name="attn_proj",
    )(x, cos, sin, *weights)


def _prefill_kernel(q_ref, k_ref, v_ref, o_ref, m_ref, acc_ref, s0_ref, s1_ref, p_ref, *, tq):
    qi = pl.program_id(2)
    q = q_ref[...]
    chunk = min(ATTN_ROW_CHUNK, tq)
    m_ref[...] = jnp.full(m_ref.shape, _NEG_INF, F32)
    acc_ref[...] = jnp.zeros(acc_ref.shape, F32)

    def scores(j, s_ref):
        start = pl.multiple_of(j * tq, tq)
        s_ref[...] = _dot_t(q, k_ref[pl.ds(start, tq), :])

    def consume(j, s_ref, masked):
        start = pl.multiple_of(j * tq, tq)
        for r0 in range(0, tq, chunk):
            rows = slice(r0, r0 + chunk)
            s = s_ref[rows, :] * EXP2_SCALE
            if masked:
                r = lax.broadcasted_iota(jnp.int32, (chunk, tq), 0) + r0
                c = lax.broadcasted_iota(jnp.int32, (chunk, tq), 1)
                s = jnp.where(c <= r, s, _NEG_INF)
            m_old = m_ref[rows, :]
            m_new = jnp.maximum(m_old, jnp.max(s, axis=-1, keepdims=True))
            p_ref[rows, :] = jnp.exp2(s - m_new).astype(BF16)
            acc_ref[rows, :] = acc_ref[rows, :] * jnp.exp2(m_old - m_new)
            m_ref[rows, :] = m_new
        acc_ref[...] += _dot(p_ref[...], v_ref[pl.ds(start, tq), :])

    scores(0, s0_ref)

    def body(i, carry):
        j = 2 * i
        scores(j + 1, s1_ref)
        consume(j, s0_ref, False)
        scores(j + 2, s0_ref)
        consume(j + 1, s1_ref, False)
        return carry

    lax.fori_loop(0, qi // 2, body, 0)

    @pl.when(qi % 2 == 0)
    def _():
        consume(qi, s0_ref, True)

    @pl.when(qi % 2 == 1)
    def _():
        scores(qi, s1_ref)
        consume(qi - 1, s0_ref, False)
        consume(qi, s1_ref, True)

    acc = acc_ref[...]
    o_ref[...] = (acc[:, :V_HEAD] / acc[:, V_HEAD:]).astype(o_ref.dtype)


def _prefill_attention(q, k, v, b, s, tq):
    q3 = q.reshape(b, s, N_HEADS * QK_PAD)
    k3 = k.reshape(b, s, N_HEADS * QK_PAD)
    v3 = v.reshape(b, s, N_HEADS * V_PAD)
    out = pl.pallas_call(
        functools.partial(_prefill_kernel, tq=tq),
        grid=(b, N_HEADS, s // tq),
        in_specs=[pl.BlockSpec((None, tq, QK_PAD), lambda bi, h, i: (bi, i, h)),
                  pl.BlockSpec((None, s, QK_PAD), lambda bi, h, i: (bi, 0, h)),
                  pl.BlockSpec((None, s, V_PAD), lambda bi, h, i: (bi, 0, h))],
        out_specs=pl.BlockSpec((None, tq, V_HEAD), lambda bi, h, i: (bi, i, h)),
        out_shape=jax.ShapeDtypeStruct((b, s, N_HEADS * V_HEAD), BF16),
        scratch_shapes=[pltpu.VMEM((tq, 1), F32), pltpu.VMEM((tq, V_PAD), F32),
                        pltpu.VMEM((tq, tq), F32), pltpu.VMEM((tq, tq), F32), pltpu.VMEM((tq, tq), BF16)],
        compiler_params=_cparams("arbitrary", "arbitrary", "arbitrary"),
        name="prefill_attention",
    )(q3, k3, v3)
    return out.reshape(b * s, N_HEADS * V_HEAD)


def _head_matmul_kernel(a_ref, w_ref, o_ref):
    o_ref[...] = _dot(a_ref[...].astype(BF16), w_ref[...]).astype(o_ref.dtype)


def _head_matmul(a, w, col_stride, out_dtype):
    n = a.shape[0]
    _, kk, mm = w.shape
    return pl.pallas_call(
        _head_matmul_kernel,
        grid=(N_HEADS,),
        in_specs=[pl.BlockSpec((n, kk), lambda h: (0, h * col_stride)),
                  pl.BlockSpec((None, kk, mm), lambda h: (h, 0, 0))],
        out_specs=pl.BlockSpec((n, mm), lambda h: (0, h)),
        out_shape=jax.ShapeDtypeStruct((n, N_HEADS * mm), out_dtype),
        compiler_params=_cparams("arbitrary"),
        name="head_matmul",
    )(a, w)


def _decode_kernel(pt_ref, ql_ref, qr_ref, cn_ref, kn_ref, *rest, pages_per_step, t):
    del pt_ref
    pg = pages_per_step
    ck_refs = rest[:pg]
    kr_refs = rest[pg:2 * pg]
    o_ref, m_ref, l_ref, acc_ref = rest[2 * pg:]
    j = pl.program_id(1)
    ql = ql_ref[...]
    qr = qr_ref[...]
    rows = ql.shape[0]

    @pl.when(j == 0)
    def _():
        cn = cn_ref[...]
        cn_b = cn.astype(BF16)
        s = (_dot_t(ql, cn_b) + _dot_t(qr, kn_ref[...].astype(BF16))) * SM_SCALE
        ti = lax.broadcasted_iota(jnp.int32, (rows, t), 0) % t
        ui = lax.broadcasted_iota(jnp.int32, (rows, t), 1)
        s = jnp.where(ui <= ti, s, _NEG_INF)
        m = jnp.max(s, axis=-1, keepdims=True)
        p = jnp.exp(s - m)
        m_ref[...] = m
        l_ref[...] = jnp.sum(p, axis=-1, keepdims=True)
        acc_ref[...] = _dot(p.astype(BF16), cn_b)

    cks = [r[...].astype(BF16) for r in ck_refs]
    s = jnp.concatenate(
        [_dot_t(ql, ck) + _dot(qr, kr[...].astype(BF16)) for ck, kr in zip(cks, kr_refs)],
        axis=-1) * SM_SCALE
    m_old = m_ref[...]
    m_new = jnp.maximum(m_old, jnp.max(s, axis=-1, keepdims=True))
    corr = jnp.exp(m_old - m_new)
    p = jnp.exp(s - m_new)
    l_ref[...] = l_ref[...] * corr + jnp.sum(p, axis=-1, keepdims=True)
    pb = p.astype(BF16)
    page = cks[0].shape[0]
    pv = _dot(pb[:, :page], cks[0])
    for r in range(1, pg):
        pv = pv + _dot(pb[:, r * page:(r + 1) * page], cks[r])
    acc_ref[...] = acc_ref[...] * corr + pv
    m_ref[...] = m_new

    @pl.when(j == pl.num_programs(1) - 1)
    def _():
        o_ref[...] = acc_ref[...] / l_ref[...]


def _decode_attention(q_lat, q_rope, ckv_new, kr_new, cache_ckv, cache_kr, page_table, pages_per_step):
    bd, rows, _ = q_lat.shape
    t = ckv_new.shape[1]
    n_pages = page_table.shape[1]
    page = cache_ckv.shape[1]
    pg = pages_per_step
    seq = lambda b, j, pt: (b, 0, 0)

    def page_map(r):
        return lambda b, j, pt: (pt[b, j * pg + r], 0, 0)

    grid_spec = pltpu.PrefetchScalarGridSpec(
        num_scalar_prefetch=1,
        grid=(bd, n_pages // pg),
        in_specs=[pl.BlockSpec((None, rows, KV_LORA), seq),
                  pl.BlockSpec((None, rows, QK_ROPE), seq),
                  pl.BlockSpec((None, t, KV_LORA), seq),
                  pl.BlockSpec((None, t, QK_ROPE), seq)]
                 + [pl.BlockSpec((None, page, KV_LORA), page_map(r)) for r in range(pg)]
                 + [pl.BlockSpec((None, QK_ROPE, page), page_map(r)) for r in range(pg)],
        out_specs=pl.BlockSpec((None, rows, KV_LORA), seq),
        scratch_shapes=[pltpu.VMEM((rows, 1), F32), pltpu.VMEM((rows, 1), F32),
                        pltpu.VMEM((rows, KV_LORA), F32)],
    )
    return pl.pallas_call(
        functools.partial(_decode_kernel, pages_per_step=pg, t=t),
        grid_spec=grid_spec,
        out_shape=jax.ShapeDtypeStruct((bd, rows, KV_LORA), F32),
        compiler_params=_cparams("arbitrary", "arbitrary"),
        name="decode_attention",
    )(page_table, q_lat, q_rope, ckv_new, kr_new, *([cache_ckv] * pg), *([cache_kr] * pg))


def _conv_tail(cv, zga, zgb, gcn_ref, bcn_ref, wpb_ref, ga_ref, hb_ref):
    act = _silu(_layernorm(cv, gcn_ref[...], bcn_ref[...])).astype(BF16)
    hb_ref[...] = _sigmoid(zgb) * _dot(act, wpb_ref[...])
    ga_ref[...] = _sigmoid(zga)


def _conv_prompt_kernel(x_ref, wcg_ref, bcg_ref, wdw_ref, bdw_ref, gcn_ref, bcn_ref, wpb_ref,
                        ga_ref, hb_ref, ulast_ref, ush_ref, cv_ref, *, tm, chunk):
    i = pl.program_id(1)
    rows = HIST + tm

    @pl.when(i == 0)
    def _():
        ush_ref[0, 0:HIST, :] = jnp.zeros((HIST, D_MODEL), F32)

    z = _dot(x_ref[...].astype(BF16), wcg_ref[...]) + bcg_ref[...]
    u = z[:, :D_MODEL] * _sigmoid(z[:, D_MODEL:2 * D_MODEL])
    ush_ref[0, HIST:rows, :] = u
    ulast_ref[...] = u[tm - HIST:tm, :]
    for p in range(1, SUBLANES):
        ush_ref[p, 0:rows - SUBLANES, :] = ush_ref[0, p:p + rows - SUBLANES, :]
    off = HIST - (CONV_W - 1)
    for c in range(tm // chunk):
        r0 = c * chunk
        acc = jnp.broadcast_to(bdw_ref[...], (chunk, D_MODEL))
        for jt in range(CONV_W):
            a, p = divmod(off + jt, SUBLANES)
            start = r0 + a * SUBLANES
            acc = acc + wdw_ref[jt:jt + 1, :] * ush_ref[p, start:start + chunk, :]
        cv_ref[r0:r0 + chunk, :] = acc
    ush_ref[0, 0:HIST, :] = ush_ref[0, tm:rows, :]
    _conv_tail(cv_ref[...], z[:, 2 * D_MODEL:3 * D_MODEL], z[:, 3 * D_MODEL:], gcn_ref, bcn_ref, wpb_ref,
               ga_ref, hb_ref)


def _conv_prompt(x3, w, tm):
    b, s, _ = x3.shape
    weights = (w["wcg"], w["b_cg"], w["w_dw"], w["b_dw"], w["g_cn"], w["b_cn"], w["wpb"])
    row = lambda bi, i: (bi, i, 0)
    return pl.pallas_call(
        functools.partial(_conv_prompt_kernel, tm=tm, chunk=32),
        grid=(b, s // tm),
        in_specs=[pl.BlockSpec((None, tm, D_MODEL), row)] + [_const_spec(a.shape) for a in weights],
        out_specs=[pl.BlockSpec((None, tm, D_MODEL), row),
                   pl.BlockSpec((None, tm, D_MODEL), row),
                   pl.BlockSpec((None, HIST, D_MODEL), lambda bi, i: (bi, 0, 0))],
        out_shape=[jax.ShapeDtypeStruct((b, s, D_MODEL), F32),
                   jax.ShapeDtypeStruct((b, s, D_MODEL), F32),
                   jax.ShapeDtypeStruct((b, HIST, D_MODEL), F32)],
        scratch_shapes=[pltpu.VMEM((SUBLANES, HIST + tm, D_MODEL), F32), pltpu.VMEM((tm, D_MODEL), F32)],
        compiler_params=_cparams("arbitrary", "arbitrary"),
        name="conv_prompt",
    )(x3, *weights)


def _conv_sample_kernel(x_ref, st_ref, wcg_ref, bcg_ref, wdw_ref, bdw_ref, gcn_ref, bcn_ref, wpb_ref,
                        ga_ref, hb_ref, u_ref, *, t):
    zs = []
    for ti in range(t):
        z = _dot(x_ref[ti].astype(BF16), wcg_ref[...]) + bcg_ref[...]
        zs.append(z)
        u_ref[ti] = z[:, :D_MODEL] * _sigmoid(z[:, D_MODEL:2 * D_MODEL])
    hist = CONV_W - 1
    for ti in range(t):
        acc = jnp.broadcast_to(bdw_ref[...], u_ref.shape[1:])
        for jt in range(CONV_W):
            p = ti + jt
            src = st_ref[p] if p < hist else u_ref[p - hist]
            acc = acc + wdw_ref[jt:jt + 1, :] * src
        z = zs[ti]
        _conv_tail(acc, z[:, 2 * D_MODEL:3 * D_MODEL], z[:, 3 * D_MODEL:], gcn_ref, bcn_ref, wpb_ref,
                   ga_ref.at[ti], hb_ref.at[ti])


def _conv_sample(x_t, state_t, w, tb):
    t, bd, _ = x_t.shape
    hist = state_t.shape[0]
    weights = (w["wcg"], w["b_cg"], w["w_dw"], w["b_dw"], w["g_cn"], w["b_cn"], w["wpb"])
    blk = lambda i: (0, i, 0)
    return pl.pallas_call(
        functools.partial(_conv_sample_kernel, t=t),
        grid=(bd // tb,),
        in_specs=[pl.BlockSpec((t, tb, D_MODEL), blk), pl.BlockSpec((hist, tb, D_MODEL), blk)]
                 + [_const_spec(a.shape) for a in weights],
        out_specs=[pl.BlockSpec((t, tb, D_MODEL), blk)] * 3,
        out_shape=[jax.ShapeDtypeStruct((t, bd, D_MODEL), F32)] * 3,
        compiler_params=_cparams("arbitrary"),
        name="conv_sample",
    )(x_t, state_t, *weights)


def _first_argmax(vals, lane):
    m = jnp.max(vals, axis=-1, keepdims=True)
    idx = jnp.min(jnp.where(vals == m, lane, float(vals.shape[-1])), axis=-1, keepdims=True)
    return m, idx


def _route(scores, bias, eidx_ref, ew_ref):
    tm = scores.shape[0]
    lane = lax.broadcasted_iota(jnp.int32, (tm, N_EXPERTS), 1).astype(F32)
    group = jnp.floor(lane * (1.0 / GROUP_SIZE))
    sel = scores + bias
    gscore = []
    for g in range(N_GROUPS):
        vals = jnp.where(group == g, sel, _NEG_INF)
        m1, i1 = _first_argmax(vals, lane)
        m2 = jnp.max(jnp.where(lane == i1, _NEG_INF, vals), axis=-1, keepdims=True)
        gscore.append(m1 + m2)
    vals = jnp.full((tm, N_EXPERTS), _NEG_INF, F32)
    for g in range(N_GROUPS):
        beaten = jnp.zeros((tm, 1), F32)
        for o in range(N_GROUPS):
            if o != g:
                ahead = gscore[o] >= gscore[g] if o < g else gscore[o] > gscore[g]
                beaten = beaten + jnp.where(ahead, 1.0, 0.0)
        kept = jnp.where(beaten < TOPK_GROUPS, 1.0, 0.0)
        vals = jnp.where((group == g) & (kept > 0.5), sel, vals)
    out_lane = lax.broadcasted_iota(jnp.int32, (tm, TOP_K), 1)
    eidx = jnp.zeros((tm, TOP_K), F32)
    ew = jnp.zeros((tm, TOP_K), F32)
    for k in range(TOP_K):
        _, ik = _first_argmax(vals, lane)
        hit = lane == ik
        wk = jnp.sum(jnp.where(hit, scores, 0.0), axis=-1, keepdims=True)
        vals = jnp.where(hit, _NEG_INF, vals)
        eidx = jnp.where(out_lane == k, ik, eidx)
        ew = jnp.where(out_lane == k, wk, ew)
    eidx_ref[...] = eidx.astype(jnp.int32)
    ew_ref[...] = ew / jnp.sum(ew, axis=-1, keepdims=True) * ROUTED_SCALE


_HI_MASK = 0xFFFF0000


def _pack_bf16_pair(x):
    half = x.shape[1] // 2
    lo = lax.bitcast_convert_type(x[:, :half].astype(BF16).astype(F32), jnp.uint32) >> 16
    hi = lax.bitcast_convert_type(x[:, half:].astype(BF16).astype(F32), jnp.uint32) & jnp.uint32(_HI_MASK)
    return lo | hi


def _unpack_bf16_pair(u):
    lo = lax.bitcast_convert_type(u << 16, F32).astype(BF16)
    hi = lax.bitcast_convert_type(u & jnp.uint32(_HI_MASK), F32).astype(BF16)
    return lo, hi


def _prompt_or_sample(p_ref, s_ref, is_sample):
    return jnp.where(is_sample, s_ref[...], p_ref[...])


def _merge_kernel(attn_p, attn_s, ga_p, ga_s, hb_p, hb_s, x_p, x_s, wpa_ref, wo_ref, g1_ref, b1_ref, wsgu_ref,
                  wsd_ref, wr_ref, br_ref, x1p_ref, base_ref, eidx_ref, ew_ref, *, alpha, prompt_tiles):
    is_sample = pl.program_id(0) >= prompt_tiles
    attn = _prompt_or_sample(attn_p, attn_s, is_sample)
    h = _prompt_or_sample(ga_p, ga_s, is_sample) * _dot(attn, wpa_ref[...]) + _prompt_or_sample(hb_p, hb_s, is_sample)
    x = _prompt_or_sample(x_p, x_s, is_sample)
    x1 = _layernorm(alpha * x + _dot(h.astype(BF16), wo_ref[...]), g1_ref[...], b1_ref[...])
    x1p_ref[...] = _pack_bf16_pair(x1)
    x1b = x1.astype(BF16)
    gu = _dot(x1b, wsgu_ref[...])
    mid = (_silu(gu[:, :D_SHARED]) * gu[:, D_SHARED:]).astype(BF16)
    base_ref[...] = alpha * x1 + _dot(mid, wsd_ref[...])
    scores = _sigmoid(_dot(x1b, wr_ref[...]))
    _route(scores, br_ref[...], eidx_ref, ew_ref)


def _split_specs(block, prompt_tiles, shift=0):
    nd = len(block)
    rest = (0,) * (nd - 1)
    tile = lambda i: jnp.maximum(i - shift, 0)
    return [pl.BlockSpec(block, lambda i: (jnp.minimum(tile(i), prompt_tiles - 1),) + rest),
            pl.BlockSpec(block, lambda i: (jnp.maximum(tile(i) - prompt_tiles, 0),) + rest)]


def _merge(attn, ga, hb, x, w, alpha, tm):
    n_p, n_s = x[0].shape[0], x[1].shape[0]
    n = n_p + n_s
    prompt_tiles = n_p // tm
    row = lambda i: (i, 0)
    weights = (w["wpa"], w["wo"], w["g_ln1"], w["b_ln1"], w["wsgu"], w["wsd"], w["wr"], w["b_router"])
    small = pl.BlockSpec((tm, TOP_K), row)
    return pl.pallas_call(
        functools.partial(_merge_kernel, alpha=alpha, prompt_tiles=prompt_tiles),
        grid=(n // tm,),
        in_specs=_split_specs((tm, D_MODEL), prompt_tiles) * 4 + [_const_spec(a.shape) for a in weights],
        out_specs=[pl.BlockSpec((tm, D_MODEL // 2), row), pl.BlockSpec((tm, D_MODEL), row), small, small],
        out_shape=[jax.ShapeDtypeStruct((n, D_MODEL // 2), jnp.uint32), jax.ShapeDtypeStruct((n, D_MODEL), F32),
                   jax.ShapeDtypeStruct((n, TOP_K), jnp.int32), jax.ShapeDtypeStruct((n, TOP_K), F32)],
        compiler_params=_cparams("arbitrary"),
        name="merge_route",
    )(*attn, *ga, *hb, *x, *weights)


def _onehots(eidx, tm):
    lane = lax.broadcasted_iota(jnp.int32, (tm, N_EXPERTS), 1)
    return [lane == eidx[:, k:k + 1] for k in range(TOP_K)]


def _rank_kernel(eidx_ref, rank_ref, count_ref, carry_ref, *, tm):
    @pl.when(pl.program_id(0) == 0)
    def _():
        carry_ref[...] = jnp.zeros(carry_ref.shape, F32)

    hits = _onehots(eidx_ref[...], tm)
    member = hits[0]
    for hk in hits[1:]:
        member = member | hk
    member_b = jnp.where(member, 1.0, 0.0).astype(BF16)
    r = lax.broadcasted_iota(jnp.int32, (tm, tm), 0)
    c = lax.broadcasted_iota(jnp.int32, (tm, tm), 1)
    below = jnp.where(c < r, 1.0, 0.0).astype(BF16)
    before = _dot(below, member_b) + carry_ref[...]
    out_lane = lax.broadcasted_iota(jnp.int32, (tm, TOP_K), 1)
    rank = jnp.zeros((tm, TOP_K), F32)
    for k in range(TOP_K):
        rk = jnp.sum(jnp.where(hits[k], before, 0.0), axis=-1, keepdims=True)
        rank = jnp.where(out_lane == k, rk, rank)
    rank_ref[...] = rank.astype(jnp.int32)
    total = carry_ref[...] + jnp.sum(member_b.astype(F32), axis=0, keepdims=True)
    carry_ref[...] = total
    count_ref[...] = total.astype(jnp.int32)


def _ranks(eidx, tm):
    n = eidx.shape[0]
    return pl.pallas_call(
        functools.partial(_rank_kernel, tm=tm),
        grid=(n // tm,),
        in_specs=[pl.BlockSpec((tm, TOP_K), lambda i: (i, 0))],
        out_specs=[pl.BlockSpec((tm, TOP_K), lambda i: (i, 0)), _const_spec((1, N_EXPERTS))],
        out_shape=[jax.ShapeDtypeStruct((n, TOP_K), jnp.int32),
                   jax.ShapeDtypeStruct((1, N_EXPERTS), jnp.int32)],
        scratch_shapes=[pltpu.VMEM((1, N_EXPERTS), F32)],
        compiler_params=_cparams("arbitrary"),
        name="expert_ranks",
    )(eidx)


def _dest_kernel(eidx_ref, rank_ref, pstart_ref, dest_ref, *, tm):
    hits = _onehots(eidx_ref[...], tm)
    pstart = pstart_ref[...]
    out_lane = lax.broadcasted_iota(jnp.int32, (tm, TOP_K), 1)
    base = jnp.zeros((tm, TOP_K), F32)
    for k in range(TOP_K):
        bk = jnp.sum(jnp.where(hits[k], pstart, 0.0), axis=-1, keepdims=True)
        base = jnp.where(out_lane == k, bk, base)
    dest_ref[...] = base.astype(jnp.int32) + rank_ref[...]


def _destinations(eidx, rank, pstart, tm):
    n = eidx.shape[0]
    row = pl.BlockSpec((tm, TOP_K), lambda i: (i, 0))
    return pl.pallas_call(
        functools.partial(_dest_kernel, tm=tm),
        grid=(n // tm,),
        in_specs=[row, row, _const_spec((1, N_EXPERTS))],
        out_specs=row,
        out_shape=jax.ShapeDtypeStruct((n, TOP_K), jnp.int32),
        compiler_params=_cparams("arbitrary"),
        name="expert_dest",
    )(eidx, rank, pstart)


def _row_copy(src_ref, src_row, dst_ref, dst_row, sem):
    return pltpu.make_async_copy(src_ref.at[pl.ds(src_row, 1), :], dst_ref.at[pl.ds(dst_row, 1), :], sem)


_PAD_PIECES = tuple(EXPERT_ROWS >> (bit + 1) for bit in range(EXPERT_ROWS.bit_length() - 1)
                    if EXPERT_ROWS >> (bit + 1) >= SUBLANES)


def _zero_fill(pad_start_ref, pad_count_ref, zero_ref, xs_hbm, sem, wait):
    def go(cp):
        if wait:
            cp.wait()
        else:
            cp.start()

    def per_expert(e, carry):
        start = pad_start_ref[e]
        lead = (-start) & (SUBLANES - 1)
        for r in range(SUBLANES - 1):
            @pl.when(r < lead)
            def _(r=r):
                go(_row_copy(zero_ref, 0, xs_hbm, start + r, sem))

        rest = pad_count_ref[e] - lead
        pos = start + lead
        for size in _PAD_PIECES:
            has = (rest & size) != 0

            @pl.when(has)
            def _(pos=pos, size=size):
                dst = xs_hbm.at[pl.ds(pl.multiple_of(pos, SUBLANES), size), :]
                go(pltpu.make_async_copy(zero_ref.at[pl.ds(0, size), :], dst, sem))

            pos = pos + jnp.where(has, size, 0)
        return carry

    lax.fori_loop(0, N_EXPERTS, per_expert, 0)

    piece = zero_ref.shape[0]

    def tail(j, carry):
        dst = xs_hbm.at[pl.ds(pl.multiple_of(j * piece, piece), piece), :]
        go(pltpu.make_async_copy(zero_ref, dst, sem))
        return carry

    end = pad_start_ref[N_EXPERTS - 1] + pad_count_ref[N_EXPERTS - 1]
    lax.fori_loop(end // piece, xs_hbm.shape[0] // piece, tail, 0)


def _wait_row_copies(src_ref, dst_ref, sem, count):
    def body(r, carry):
        for _ in range(TOP_K):
            _row_copy(src_ref, 0, dst_ref, 0, sem).wait()
        return carry

    lax.fori_loop(0, count, body, 0)


def _dispatch_kernel(pad_start_ref, pad_count_ref, dest_hbm, x_ref, xs_hbm, dest_smem, zero_ref, isem, sem,
                     zsem, *, tm):
    i = pl.program_id(0)
    last = pl.num_programs(0) - 1
    per_tile = tm * TOP_K
    slot = i % 2

    def idx_copy(step, s):
        dst = dest_smem.at[pl.ds(pl.multiple_of(s * per_tile, per_tile), per_tile)]
        return pltpu.make_async_copy(dest_hbm.at[step], dst, isem.at[s])

    @pl.when(i == 0)
    def _():
        zero_ref[...] = jnp.zeros(zero_ref.shape, zero_ref.dtype)
        _zero_fill(pad_start_ref, pad_count_ref, zero_ref, xs_hbm, zsem, wait=False)
        idx_copy(0, 0).start()

    @pl.when(i < last)
    def _():
        idx_copy(i + 1, 1 - slot).start()

    idx_copy(i, slot).wait()
    first = slot * per_tile

    def issue(r, carry):
        for k in range(TOP_K):
            _row_copy(x_ref, r, xs_hbm, dest_smem[first + r * TOP_K + k], sem).start(priority=k % 2)
        return carry

    lax.fori_loop(0, tm, issue, 0)
    _wait_row_copies(x_ref, xs_hbm, sem, tm)

    @pl.when(i == last)
    def _():
        _zero_fill(pad_start_ref, pad_count_ref, zero_ref, xs_hbm, zsem, wait=True)


def _dispatch(dest, x1p, pad_start, pad_count, rows, tm):
    n, width = x1p.shape
    dest2 = dest.reshape(n // tm, tm * TOP_K)
    grid_spec = pltpu.PrefetchScalarGridSpec(
        num_scalar_prefetch=2,
        grid=(n // tm,),
        in_specs=[pl.BlockSpec(memory_space=pl.ANY),
                  pl.BlockSpec((tm, width), lambda i, ps, pc: (i, 0))],
        out_specs=pl.BlockSpec(memory_space=pl.ANY),
        scratch_shapes=[pltpu.SMEM((2 * tm * TOP_K,), jnp.int32), pltpu.VMEM((_PAD_PIECES[0], width), x1p.dtype),
                        pltpu.SemaphoreType.DMA((2,)), pltpu.SemaphoreType.DMA, pltpu.SemaphoreType.DMA],
    )
    return pl.pallas_call(
        functools.partial(_dispatch_kernel, tm=tm),
        grid_spec=grid_spec,
        out_shape=jax.ShapeDtypeStruct((rows, width), x1p.dtype),
        compiler_params=_cparams("arbitrary"),
        name="dispatch",
    )(pad_start, pad_count, dest2, x1p)


def _expert_kernel(blk_e_ref, nused_ref, xs_ref, wg_ref, wu_ref, wd_ref, y_ref):
    del blk_e_ref
    g = pl.program_id(0)

    @pl.when(g < nused_ref[0])
    def _():
        lo, hi = _unpack_bf16_pair(xs_ref[...])
        half = D_MODEL // 2

        def proj(w_ref):
            return _dot(lo, w_ref[:half, :].astype(BF16)) + _dot(hi, w_ref[half:, :].astype(BF16))

        mid = (_silu(proj(wg_ref)) * proj(wu_ref)).astype(BF16)
        y_ref[...] = _dot(mid, wd_ref[...].astype(BF16))

    @pl.when(g >= nused_ref[0])
    def _():
        y_ref[...] = jnp.zeros(y_ref.shape, F32)


def _expert_ffn(blk_e, nused, xs, w_eg, w_eu, w_ed):
    rows = xs.shape[0]
    n_blocks = rows // EXPERT_ROWS
    row = lambda g, be, nu: (g, 0)
    in_row = lambda g, be, nu: (jnp.minimum(g, nu[0] - 1), 0)
    wsel = lambda g, be, nu: (be[g], 0, 0)
    grid_spec = pltpu.PrefetchScalarGridSpec(
        num_scalar_prefetch=2,
        grid=(n_blocks,),
        in_specs=[pl.BlockSpec((EXPERT_ROWS, xs.shape[1]), in_row),
                  pl.BlockSpec((None, D_MODEL, D_EXPERT), wsel),
                  pl.BlockSpec((None, D_MODEL, D_EXPERT), wsel),
                  pl.BlockSpec((None, D_EXPERT, D_MODEL), wsel)],
        out_specs=pl.BlockSpec((EXPERT_ROWS, D_MODEL), row),
    )
    return pl.pallas_call(
        _expert_kernel,
        grid_spec=grid_spec,
        out_shape=jax.ShapeDtypeStruct((rows, D_MODEL), F32),
        compiler_params=_cparams("arbitrary"),
        name="expert_ffn",
    )(blk_e, nused, xs, w_eg, w_eu, w_ed)


def _combine_kernel(dest_hbm, y_hbm, ew_ref, base_ref, pp_ref, ps_ref, g2_ref, b2_ref, wple_ref, wpleg_ref,
                    out_ref, dest_smem, ybuf_ref, isem, sem, *, tm, prompt_tiles):
    j = pl.program_id(0)
    tiles = pl.num_programs(0) - 1
    per_tile = tm * TOP_K

    def idx_copy(tile):
        s = tile % 2
        dst = dest_smem.at[pl.ds(pl.multiple_of(s * per_tile, per_tile), per_tile)]
        return pltpu.make_async_copy(dest_hbm.at[tile], dst, isem.at[s])

    @pl.when(j == 0)
    def _():
        idx_copy(0).start()

    @pl.when(j + 1 < tiles)
    def _():
        idx_copy(j + 1).start()

    @pl.when(j < tiles)
    def _():
        idx_copy(j).wait()
        s = j % 2
        first = s * per_tile

        def issue(r, carry):
            for k in range(TOP_K):
                _row_copy(y_hbm, dest_smem[first + r * TOP_K + k], ybuf_ref.at[s, k], r,
                          sem.at[s]).start(priority=k % 2)
            return carry

        lax.fori_loop(0, tm, issue, 0)

    @pl.when(j > 0)
    def _():
        c = j - 1
        s = c % 2
        _wait_row_copies(y_hbm, ybuf_ref.at[s, 0], sem.at[s], tm)
        ew = ew_ref[...]
        routed = ew[:, 0:1] * ybuf_ref[s, 0]
        for k in range(1, TOP_K):
            routed = routed + ew[:, k:k + 1] * ybuf_ref[s, k]
        x2 = _layernorm(base_ref[...] + routed, g2_ref[...], b2_ref[...])
        p = _prompt_or_sample(pp_ref, ps_ref, c >= prompt_tiles)
        emb = _dot(p.astype(BF16), wple_ref[...])
        out_ref[...] = x2 + emb * _sigmoid(_dot(x2.astype(BF16), wpleg_ref[...]))


def _combine(dest, y, ew, base, p, w, tm):
    n = base.shape[0]
    prompt_tiles = p[0].shape[0] // tm
    dest2 = dest.reshape(n // tm, tm * TOP_K)
    row = lambda j: (jnp.maximum(j - 1, 0), 0)
    weights = (w["g_ln2"], w["b_ln2"], w["wple"], w["wpleg"])
    return pl.pallas_call(
        functools.partial(_combine_kernel, tm=tm, prompt_tiles=prompt_tiles),
        grid=(n // tm + 1,),
        in_specs=[pl.BlockSpec(memory_space=pl.ANY), pl.BlockSpec(memory_space=pl.ANY),
                  pl.BlockSpec((tm, TOP_K), row), pl.BlockSpec((tm, D_MODEL), row)]
                 + _split_specs((tm, PLE_DIM), prompt_tiles, shift=1) + [_const_spec(a.shape) for a in weights],
        out_specs=pl.BlockSpec((tm, D_MODEL), row),
        out_shape=jax.ShapeDtypeStruct((n, D_MODEL), F32),
        scratch_shapes=[pltpu.SMEM((2 * tm * TOP_K,), jnp.int32), pltpu.VMEM((2, TOP_K, tm, D_MODEL), F32),
                        pltpu.SemaphoreType.DMA((2,)), pltpu.SemaphoreType.DMA((2,))],
        compiler_params=_cparams("arbitrary"),
        name="combine",
    )(dest2, y, ew, base, *p, *weights)


def _rope_tables(pos):
    half = QK_ROPE // 2
    inv = ROPE_THETA ** (-jnp.arange(half, dtype=F32) / half)
    ang = pos.astype(F32)[:, None] * inv[None, :]
    reps = LANES // half
    return jnp.tile(jnp.cos(ang), (1, reps)), jnp.tile(jnp.sin(ang), (1, reps))


def _swap_halves(w):
    half = w.shape[-1] // 2
    return jnp.concatenate([-w[..., half:], w[..., :half]], axis=-1)


def _layer_weights(i, w_in, b_cg, g_qn, w_uq, g_kvn, w_uk, w_uv, w_pa, w_dw, b_dw, g_cn, b_cn, w_pb, w_o,
                   g_ln1, b_ln1, w_router, b_router, w_sg, w_su, w_sd, g_ln2, b_ln2, w_ple, w_pleg):
    o1, o2, o3 = Q_LORA, Q_LORA + KV_LORA, Q_LORA + KV_LORA + QK_ROPE
    win = w_in[i]
    row = lambda v: v[i].reshape(1, -1)
    wkr = win[:, o2:o3]
    zpad = jnp.zeros((D_MODEL, LANES - QK_ROPE), F32)
    uq = w_uq[i].reshape(Q_LORA, N_HEADS, QK_NOPE + QK_ROPE)
    uq_n, uq_r = uq[..., :QK_NOPE], uq[..., QK_NOPE:]
    hpad = jnp.zeros((Q_LORA, N_HEADS, LANES - QK_ROPE), F32)
    return {
        "wq": win[:, :o1].astype(BF16),
        "wkv": win[:, o1:o2].astype(BF16),
        "wkr": jnp.concatenate([wkr, zpad, _swap_halves(wkr), zpad], axis=1).astype(BF16),
        "wcg": win[:, o3:].astype(BF16),
        "b_cg": row(b_cg),
        "g_qn": row(g_qn),
        "g_kvn": row(g_kvn),
        "wuq": jnp.concatenate([uq_n, uq_r, hpad], axis=-1).reshape(Q_LORA, N_HEADS * QK_PAD).astype(BF16),
        "wuqs": jnp.concatenate([_swap_halves(uq_r), hpad], axis=-1).reshape(Q_LORA, N_HEADS * LANES).astype(BF16),
        "wuk": w_uk[i].reshape(KV_LORA, N_HEADS * QK_NOPE).astype(BF16),
        "wuv": w_uv[i].reshape(KV_LORA, N_HEADS * V_HEAD).astype(BF16),
        "wuk_t": w_uk[i].transpose(1, 2, 0).astype(BF16),
        "wuv_h": w_uv[i].transpose(1, 0, 2).astype(BF16),
        "wpa": w_pa[i].astype(BF16),
        "w_dw": w_dw[i],
        "b_dw": row(b_dw),
        "g_cn": row(g_cn),
        "b_cn": row(b_cn),
        "wpb": w_pb[i].astype(BF16),
        "wo": w_o[i].astype(BF16),
        "g_ln1": row(g_ln1),
        "b_ln1": row(b_ln1),
        "wr": w_router[i].astype(BF16),
        "b_router": row(b_router),
        "wsgu": jnp.concatenate([w_sg[i], w_su[i]], axis=1).astype(BF16),
        "wsd": w_sd[i].astype(BF16),
        "g_ln2": row(g_ln2),
        "b_ln2": row(b_ln2),
        "wple": w_ple[i].astype(BF16),
        "wpleg": w_pleg[i].astype(BF16),
    }


def _pick_tile(n, target):
    tm = min(n, target)
    while n % tm:
        tm //= 2
    return tm


def kernel(x_prompt, x_sample, cache_ckv, cache_krope, state_conv, page_table, p_prompt, p_sample, w_in, b_cg, g_qn, w_uq, g_kvn, w_uk, w_uv, w_pa, w_dw, b_dw, g_cn, b_cn, w_pb, w_o, g_ln1, b_ln1, w_router, b_router, w_eg, w_eu, w_ed, w_sg, w_su, w_sd, g_ln2, b_ln2, w_ple, w_pleg):
    b, s, d = x_prompt.shape
    bd, t, _ = x_sample.shape
    depth = w_in.shape[0]
    n_p, n_s = b * s, bd * t
    n = n_p + n_s
    past = page_table.shape[1] * cache_ckv.shape[2]
    alpha = (2 * depth) ** 0.25
    hist = CONV_W - 1

    cos_p, sin_p = _rope_tables(jnp.arange(s, dtype=jnp.int32))
    cos_s, sin_s = _rope_tables(past + jnp.tile(jnp.arange(t, dtype=jnp.int32), bd))

    tm_p = _pick_tile(s, 256)
    tm_s = _pick_tile(n_s, 256)
    common = math.gcd(n_p, n_s)
    tm_n = _pick_tile(common, 512)
    tm_m = _pick_tile(common, 512)
    tm_g = _pick_tile(common, 128)
    tq = _pick_tile(s, 512)
    pages_per_step = _pick_tile(page_table.shape[1], 16)
    n_blocks = -(-(n * TOP_K + N_EXPERTS * (EXPERT_ROWS - 1)) // EXPERT_ROWS)

    xp = x_prompt.reshape(n_p, d)
    xs = x_sample.reshape(n_s, d)
    outs = {k: [] for k in ("ckv_p", "kr_p", "conv_p", "ckv_s", "kr_s", "conv_s")}
    for i in range(depth):
        w = _layer_weights(i, w_in, b_cg, g_qn, w_uq, g_kvn, w_uk, w_uv, w_pa, w_dw, b_dw, g_cn, b_cn, w_pb,
                           w_o, g_ln1, b_ln1, w_router, b_router, w_sg, w_su, w_sd, g_ln2, b_ln2, w_ple, w_pleg)
        ckv_p, kr_p, q_p, k_p, v_p = _attn_proj(xp, cos_p, sin_p, w, tm_p)
        attn_p = _prefill_attention(q_p, k_p, v_p, b, s, tq)
        ckv_s, kr_s, q_s, _, _ = _attn_proj(xs, cos_s, sin_s, w, tm_s)
        q_lat = _head_matmul(q_s, w["wuk_t"], 2, BF16)
        q_lat = q_lat.reshape(bd, t, N_HEADS, KV_LORA).transpose(0, 2, 1, 3).reshape(bd, N_HEADS * t, KV_LORA)
        q_r = q_s.reshape(bd, t, N_HEADS, QK_PAD)[..., QK_NOPE:QK_NOPE + QK_ROPE]
        q_r = q_r.transpose(0, 2, 1, 3).reshape(bd, N_HEADS * t, QK_ROPE)
        o_lat = _decode_attention(q_lat, q_r, ckv_s.reshape(bd, t, KV_LORA), kr_s.reshape(bd, t, QK_ROPE),
                                  cache_ckv[i], jnp.swapaxes(cache_krope[i], 1, 2), page_table, pages_per_step)
        o_lat = o_lat.reshape(bd, N_HEADS, t, KV_LORA).transpose(0, 2, 1, 3).reshape(n_s, N_HEADS * KV_LORA)
        attn_s = _head_matmul(o_lat, w["wuv_h"], 1, BF16)
        ga_p, hb_p, ulast = _conv_prompt(xp.reshape(b, s, d), w, tm_p)
        state_t = state_conv[i].transpose(1, 0, 2)
        ga_s, hb_s, u_s = _conv_sample(xs.reshape(bd, t, d).transpose(1, 0, 2), state_t, w, _pick_tile(bd, 128))
        from_t = lambda a: a.transpose(1, 0, 2).reshape(n_s, d)
        x1p, base, eidx, ew = _merge((attn_p, attn_s), (ga_p.reshape(n_p, d), from_t(ga_s)),
                                     (hb_p.reshape(n_p, d), from_t(hb_s)), (xp, xs), w, alpha, tm_m)
        rank, counts = _ranks(eidx, tm_n)
        pcounts = (counts[0] + EXPERT_ROWS - 1) // EXPERT_ROWS * EXPERT_ROWS
        pends = jnp.cumsum(pcounts)
        pstart = (pends - pcounts).astype(jnp.int32)
        nused = (pends[-1] // EXPERT_ROWS).astype(jnp.int32)
        blk = jnp.minimum(jnp.arange(n_blocks, dtype=jnp.int32), nused - 1) * EXPERT_ROWS
        blk_e = jnp.minimum(jnp.sum(pends[None, :] <= blk[:, None], axis=1), N_EXPERTS - 1).astype(jnp.int32)
        dest = _destinations(eidx, rank, pstart.astype(F32).reshape(1, N_EXPERTS), tm_n)
        xsorted = _dispatch(dest, x1p, pstart + counts[0], pcounts - counts[0], n_blocks * EXPERT_ROWS, tm_g)
        y = _expert_ffn(blk_e, nused.reshape(1), xsorted, w_eg[i], w_eu[i], w_ed[i])
        p_pair = (p_prompt[i].reshape(n_p, PLE_DIM), p_sample[i].reshape(n_s, PLE_DIM))
        x_all = _combine(dest, y, ew, base, p_pair, w, tm_g)
        xp, xs = x_all[:n_p], x_all[n_p:]
        outs["ckv_p"].append(ckv_p.reshape(b, s, KV_LORA))
        outs["kr_p"].append(kr_p.reshape(b, s, QK_ROPE))
        outs["conv_p"].append(ulast[:, HIST - hist:, :])
        outs["ckv_s"].append(ckv_s.reshape(bd, t, KV_LORA))
        outs["kr_s"].append(kr_s.reshape(bd, t, QK_ROPE))
        outs["conv_s"].append(jnp.concatenate([state_conv[i], u_s.transpose(1, 0, 2)], axis=1)[:, -hist:])
    return (xp.reshape(b, s, d), xs.reshape(bd, t, d), jnp.stack(outs["ckv_p"]), jnp.stack(outs["kr_p"]),
            jnp.stack(outs["conv_p"]), jnp.stack(outs["ckv_s"]), jnp.stack(outs["kr_s"]),
            jnp.stack(outs["conv_s"]))
```
